```python
import jax, jax.numpy as jnp
from jax import lax
import numpy as np

D_MODEL = 1024
BATCH = 32
SEQ = 2048
DEPTH = 1
DEC_BATCH = 16
DEC_SEQ = 32
PAST_LEN = 1024

CHUNK = 64
N_HEADS = 16
HEAD_DIM = 64
ATTN_WIDTH = N_HEADS * HEAD_DIM
CONV_WIDTH = D_MODEL
CONV_K = 31
CONV_HIST = CONV_K - 1
Q_BLOCK = 128
EPS = 1e-6

OFF_Q = 0
OFF_K = OFF_Q + ATTN_WIDTH
OFF_V = OFF_K + ATTN_WIDTH
OFF_F = OFF_V + ATTN_WIDTH
OFF_GA = OFF_F + N_HEADS
OFF_UA = OFF_GA + ATTN_WIDTH
OFF_UB = OFF_UA + CONV_WIDTH
OFF_GB = OFF_UB + CONV_WIDTH
OFF_MA = OFF_GB + CONV_WIDTH
OFF_MB = OFF_MA + D_MODEL
IN_WIDTH = OFF_MB + D_MODEL

kernel_name = "fox_conformer_gated_hybrid_step"


def rmsnorm(x, g):
    xf = x.astype(jnp.float32)
    y = xf * lax.rsqrt(jnp.mean(xf * xf, axis=-1, keepdims=True) + EPS) * g.astype(jnp.float32)
    return y.astype(x.dtype)


def layernorm(x, g, b):
    xf = x.astype(jnp.float32)
    mu = jnp.mean(xf, axis=-1, keepdims=True)
    xc = xf - mu
    y = xc * lax.rsqrt(jnp.mean(xc * xc, axis=-1, keepdims=True) + EPS) * g.astype(jnp.float32) + b.astype(jnp.float32)
    return y.astype(x.dtype)


def mixer_inputs(x, norm_g, w_in, b_f, q_g, k_g):
    B, T, _ = x.shape
    h = rmsnorm(x, norm_g)
    cols = lambda a, w: h @ w_in[:, a:a + w]
    q = rmsnorm(cols(OFF_Q, ATTN_WIDTH).reshape(B, T, N_HEADS, HEAD_DIM), q_g) * (HEAD_DIM ** -0.5)
    k = rmsnorm(cols(OFF_K, ATTN_WIDTH).reshape(B, T, N_HEADS, HEAD_DIM), k_g)
    v = cols(OFF_V, ATTN_WIDTH).reshape(B, T, N_HEADS, HEAD_DIM)
    logf = jax.nn.log_sigmoid((cols(OFF_F, N_HEADS) + b_f).astype(jnp.float32))
    ga = cols(OFF_GA, ATTN_WIDTH)
    u = cols(OFF_UA, CONV_WIDTH) * jax.nn.sigmoid(cols(OFF_UB, CONV_WIDTH))
    gb = cols(OFF_GB, CONV_WIDTH)
    ma = cols(OFF_MA, D_MODEL)
    mb = cols(OFF_MB, D_MODEL)
    return q, k, v, logf, ga, u, gb, ma, mb


def fox_attend(q, k, v, cq, ck, qpos, kpos):
    s = jnp.einsum('bqhd,bkhd->bhqk', q.astype(jnp.float32), k.astype(jnp.float32))
    bias = jnp.transpose(cq, (0, 2, 1))[:, :, :, None] - jnp.transpose(ck, (0, 2, 1))[:, :, None, :]
    mask = qpos[:, None] >= kpos[None, :]
    s = jnp.where(mask[None, None], s + bias, jnp.finfo(jnp.float32).min)
    p = jax.nn.softmax(s, axis=-1)
    o = jnp.einsum('bhqk,bkhd->bqhd', p, v.astype(jnp.float32))
    return o.astype(v.dtype)


def causal_dwconv(u_ext, w_dw, b_dw):
    y = lax.conv_general_dilated(u_ext, w_dw[:, None, :], window_strides=(1,), padding='VALID',
                                 dimension_numbers=('NWC', 'WIO', 'NWC'), feature_group_count=CONV_WIDTH)
    return y + b_dw


def mixer_outputs(x, o_attn, conv, ga, gb, ma, mb, ln_g, ln_b, w_pa, w_pb, w_out):
    B, T, _ = x.shape
    ya = (o_attn.reshape(B, T, ATTN_WIDTH) * jax.nn.silu(ga)) @ w_pa
    yb = (jax.nn.silu(layernorm(conv, ln_g, ln_b)) * jax.nn.silu(gb)) @ w_pb
    m = jax.nn.sigmoid(ma) * ya + jax.nn.sigmoid(mb) * yb
    return x + m @ w_out


def setup_inputs(seed: int = 0) -> dict:
    key = jax.random.key(seed)
    ks = jax.random.split(key, 20)
    nrm = lambda k, shape, scale: jax.random.normal(k, shape, jnp.float32) * scale
    return {
        "x_prompt": nrm(ks[0], (BATCH, SEQ, D_MODEL), 1.0),
        "x_sample": nrm(ks[1], (DEC_BATCH, DEC_SEQ, D_MODEL), 1.0),
        "cache_k": nrm(ks[2], (DEPTH, DEC_BATCH, PAST_LEN, N_HEADS, HEAD_DIM), 1.0),
        "cache_v": nrm(ks[3], (DEPTH, DEC_BATCH, PAST_LEN, N_HEADS, HEAD_DIM), 1.0),
        "cache_logf": jax.nn.log_sigmoid(nrm(ks[4], (DEPTH, DEC_BATCH, PAST_LEN, N_HEADS), 1.0) + 2.5),
        "state_conv": nrm(ks[5], (DEPTH, DEC_BATCH, CONV_HIST, CONV_WIDTH), 0.5),
        "norm_g": 1.0 + nrm(ks[6], (DEPTH, D_MODEL), 0.02),
        "w_in": nrm(ks[7], (DEPTH, D_MODEL, IN_WIDTH), D_MODEL ** -0.5),
        "b_f": jax.random.uniform(ks[8], (DEPTH, N_HEADS), jnp.float32, 1.0, 4.0),
        "q_g": 1.0 + nrm(ks[9], (DEPTH, HEAD_DIM), 0.02),
        "k_g": 1.0 + nrm(ks[10], (DEPTH, HEAD_DIM), 0.02),
        "w_dw": nrm(ks[11], (DEPTH, CONV_K, CONV_WIDTH), CONV_K ** -0.5),
        "b_dw": nrm(ks[12], (DEPTH, CONV_WIDTH), 0.02),
        "ln_g": 1.0 + nrm(ks[13], (DEPTH, CONV_WIDTH), 0.02),
        "ln_b": nrm(ks[14], (DEPTH, CONV_WIDTH), 0.02),
        "w_pa": nrm(ks[15], (DEPTH, ATTN_WIDTH, D_MODEL), ATTN_WIDTH ** -0.5),
        "w_pb": nrm(ks[16], (DEPTH, CONV_WIDTH, D_MODEL), CONV_WIDTH ** -0.5),
        "w_out": nrm(ks[17], (DEPTH, D_MODEL, D_MODEL), D_MODEL ** -0.5),
    }


def reference(x_prompt, x_sample, cache_k, cache_v, cache_logf, state_conv, norm_g, w_in, b_f, q_g, k_g,
              w_dw, b_dw, ln_g, ln_b, w_pa, w_pb, w_out):
    xp, xs = x_prompt, x_sample
    kp_l, vp_l, fp_l, cp_l, ks_l, vs_l, fs_l, cs_l = [], [], [], [], [], [], [], []
    for l in range(DEPTH):
        q, k, v, lf, ga, u, gb, ma, mb = mixer_inputs(xp, norm_g[l], w_in[l], b_f[l], q_g[l], k_g[l])
        T = xp.shape[1]
        c = jnp.cumsum(lf, axis=1)
        blocks = []
        for i in range(T // Q_BLOCK):
            lo, hi = i * Q_BLOCK, (i + 1) * Q_BLOCK
            blocks.append(fox_attend(q[:, lo:hi], k[:, :hi], v[:, :hi], c[:, lo:hi], c[:, :hi],
                                     jnp.arange(lo, hi), jnp.arange(hi)))
        o = jnp.concatenate(blocks, axis=1)
        u_ext = jnp.pad(u, ((0, 0), (CONV_HIST, 0), (0, 0)))
        conv = causal_dwconv(u_ext, w_dw[l], b_dw[l])
        xp = mixer_outputs(xp, o, conv, ga, gb, ma, mb, ln_g[l], ln_b[l], w_pa[l], w_pb[l], w_out[l])
        kp_l.append(k); vp_l.append(v); fp_l.append(lf); cp_l.append(u_ext[:, -CONV_HIST:])

        qs, kn, vn, lfn, gas, us, gbs, mas, mbs = mixer_inputs(xs, norm_g[l], w_in[l], b_f[l], q_g[l], k_g[l])
        P = cache_k.shape[2]
        Ts = xs.shape[1]
        k_all = jnp.concatenate([cache_k[l], kn], axis=1)
        v_all = jnp.concatenate([cache_v[l], vn], axis=1)
        lf_all = jnp.concatenate([cache_logf[l].astype(jnp.float32), lfn], axis=1)
        c_all = jnp.cumsum(lf_all, axis=1)
        os_ = fox_attend(qs, k_all, v_all, c_all[:, P:], c_all, jnp.arange(P, P + Ts), jnp.arange(P + Ts))
        us_ext = jnp.concatenate([state_conv[l].astype(us.dtype), us], axis=1)
        convs = causal_dwconv(us_ext, w_dw[l], b_dw[l])
        xs = mixer_outputs(xs, os_, convs, gas, gbs, mas, mbs, ln_g[l], ln_b[l], w_pa[l], w_pb[l], w_out[l])
        ks_l.append(kn); vs_l.append(vn); fs_l.append(lfn); cs_l.append(us_ext[:, -CONV_HIST:])

    return (xp, xs,
            jnp.stack(kp_l), jnp.stack(vp_l), jnp.stack(fp_l), jnp.stack(cp_l),
            jnp.stack(ks_l), jnp.stack(vs_l), jnp.stack(fs_l), jnp.stack(cs_l))
```

```python
import functools

import jax
import jax.numpy as jnp
from jax import lax
from jax.experimental import pallas as pl
from jax.experimental.pallas import tpu as pltpu

F32 = jnp.float32
BF16 = jnp.bfloat16

D_MODEL = 1024
N_HEADS = 16
HEAD_DIM = 64
CONV_K = 31
CONV_HIST = CONV_K - 1
EPS = 1e-6

LANES = 128
HALO = 32
N_LANE_CHUNKS = D_MODEL // LANES
HEADS_PER_BLOCK = LANES // HEAD_DIM
N_HEAD_BLOCKS = N_HEADS // HEADS_PER_BLOCK
MXU_DIM = 256
N_SECTIONS = 9
W_COLS = N_SECTIONS * D_MODEL + LANES
MASK_VALUE = -1e30
VMEM_LIMIT_BYTES = 56 * 1024 * 1024

ROW_TILE = 512
Q_TILE = 256


def _sigmoid(x):
    return 1.0 / (1.0 + jnp.exp(-x))


def _dot(a, b):
    return jnp.dot(a, b, preferred_element_type=F32)


def _dot_nt(a, b):
    return lax.dot_general(a, b, (((1,), (1,)), ((), ())), preferred_element_type=F32)


def _inproj_kernel(x_ref, ng_ref, w_ref, bf_ref, qg_ref, kg_ref, e_ref, et_ref,
                   q_ref, k_ref, v_ref, lf_ref, lft_ref, sga_ref, u_ref, utail_ref,
                   sgb_ref, sma_ref, smb_ref, *, seg):
    x = x_ref[...]
    ms = jnp.mean(x * x, axis=-1, keepdims=True)
    h = (x * lax.rsqrt(ms + EPS) * ng_ref[...]).astype(BF16)

    def proj(sec):
        return _dot(h, w_ref[:, sec * D_MODEL:(sec + 1) * D_MODEL])

    def head_rmsnorm(y, g):
        ss = _dot((y * y).astype(BF16), e_ref[...])
        inv = lax.rsqrt(ss * (1.0 / HEAD_DIM) + EPS)
        hi = inv.astype(BF16)
        lo = (inv - hi.astype(F32)).astype(BF16)
        scale = _dot(jnp.concatenate([hi, lo], axis=1), et_ref[...])
        return y * scale * g

    q_ref[...] = head_rmsnorm(proj(0), qg_ref[...]).astype(BF16)
    k_ref[...] = head_rmsnorm(proj(1), kg_ref[...])
    v_ref[...] = proj(2)

    z = _dot(h, w_ref[:, N_SECTIONS * D_MODEL:]) + bf_ref[...]
    lf = jnp.minimum(z, 0.0) - jnp.log1p(jnp.exp(-jnp.abs(z)))
    lf_ref[...] = lf[:, :N_HEADS]
    lft_ref[0] = lf.T[:N_HEADS, :]

    ga = proj(3)
    sga_ref[...] = (ga * _sigmoid(ga)).astype(BF16)
    u = proj(4) * _sigmoid(proj(5))
    u_ref[...] = u.astype(BF16)
    for s in range(utail_ref.shape[0]):
        utail_ref[s] = u[(s + 1) * seg - HALO:(s + 1) * seg]
    gb = proj(6)
    sgb_ref[...] = (gb * _sigmoid(gb)).astype(BF16)
    sma_ref[...] = _sigmoid(proj(7)).astype(BF16)
    smb_ref[...] = _sigmoid(proj(8)).astype(BF16)


def _inproj(x2d, seq_len, ng, w_all, bf_pad, qg, kg, e_mat, et_mat):
    rows = x2d.shape[0]
    tm = min(ROW_TILE, rows)
    n_tiles = rows // tm
    n_seq = rows // seq_len
    seg = min(seq_len, tm)
    seq_per_tile = tm // seg
    tiles_per_seq = seq_len // seg

    row_spec = pl.BlockSpec((tm, D_MODEL), lambda i: (i, 0))
    const = lambda shape: pl.BlockSpec(shape, lambda i: (0,) * len(shape))
    if tiles_per_seq > 1:
        lft_shape = (n_seq, N_HEADS, seq_len)
        lft_spec = pl.BlockSpec((1, N_HEADS, tm), lambda i: (i // tiles_per_seq, 0, i % tiles_per_seq))
    else:
        lft_shape = (n_tiles, N_HEADS, tm)
        lft_spec = pl.BlockSpec((1, N_HEADS, tm), lambda i: (i, 0, 0))
    utail_spec = pl.BlockSpec((seq_per_tile, HALO, D_MODEL),
                              lambda i: (i * seq_per_tile // tiles_per_seq, 0, 0))
    bf_rows = jax.ShapeDtypeStruct((rows, D_MODEL), BF16)
    f32_rows = jax.ShapeDtypeStruct((rows, D_MODEL), F32)
    return pl.pallas_call(
        functools.partial(_inproj_kernel, seg=seg),
        grid=(n_tiles,),
        in_specs=[row_spec, const((1, D_MODEL)),
                  pl.BlockSpec((D_MODEL, W_COLS), lambda i: (0, 0), pipeline_mode=pl.Buffered(1)),
                  const((1, LANES)), const((1, D_MODEL)), const((1, D_MODEL)),
                  const((D_MODEL, LANES)), const((2 * LANES, D_MODEL))],
        out_specs=[row_spec, row_spec, row_spec,
                   pl.BlockSpec((tm, N_HEADS), lambda i: (i, 0)), lft_spec,
                   row_spec, row_spec, utail_spec, row_spec, row_spec, row_spec],
        out_shape=[bf_rows, f32_rows, f32_rows,
                   jax.ShapeDtypeStruct((rows, N_HEADS), F32),
                   jax.ShapeDtypeStruct(lft_shape, F32),
                   bf_rows, bf_rows,
                   jax.ShapeDtypeStruct((n_seq, HALO, D_MODEL), F32),
                   bf_rows, bf_rows, bf_rows],
        compiler_params=pltpu.CompilerParams(dimension_semantics=("arbitrary",),
                                             vmem_limit_bytes=VMEM_LIMIT_BYTES),
        name="inproj",
    )(x2d, ng, w_all, bf_pad, qg, kg, e_mat, et_mat)


def _neg_cumsum_blocks(lf):
    hi = lf.astype(BF16)
    r1 = lf - hi.astype(F32)
    mid = r1.astype(BF16)
    lo = (r1 - mid.astype(F32)).astype(BF16)
    row = lax.broadcasted_iota(jnp.int32, (MXU_DIM, MXU_DIM), 0)
    col = lax.broadcasted_iota(jnp.int32, (MXU_DIM, MXU_DIM), 1)
    tri = (row <= col).astype(BF16)
    carry = jnp.zeros((lf.shape[0], 1), F32)
    out = []
    for b in range(lf.shape[1] // MXU_DIM):
        sl = slice(b * MXU_DIM, (b + 1) * MXU_DIM)
        c = _dot(hi[:, sl], tri) + _dot(mid[:, sl], tri) + _dot(lo[:, sl], tri) + carry
        out.append(-c)
        carry = c[:, MXU_DIM - 1:MXU_DIM]
    return out


def _stack_heads(q):
    lane = lax.broadcasted_iota(jnp.int32, q.shape, 1)
    zero = jnp.zeros_like(q)
    return jnp.concatenate([jnp.where(lane < HEAD_DIM, q, zero),
                            jnp.where(lane >= HEAD_DIM, q, zero)], axis=0)


def _unstack_heads(o, tq):
    lane = lax.broadcasted_iota(jnp.int32, (tq, LANES), 1)
    return jnp.where(lane < HEAD_DIM, o[:tq], o[tq:])


def _causal_mask(s, tq):
    row = lax.broadcasted_iota(jnp.int32, s.shape, 0)
    row = jnp.where(row >= tq, row - tq, row)
    col = lax.broadcasted_iota(jnp.int32, s.shape, 1)
    return jnp.where(col <= row, s, MASK_VALUE)


def _attn_prompt_kernel(q_ref, k_ref, v_ref, lft_ref, sga_ref, o_ref,
                        kb_ref, vb_ref, negc_ref, m_ref, l_ref, acc_ref):
    hp = pl.program_id(1)
    qi = pl.program_id(2)
    tq = q_ref.shape[1]

    @pl.when((hp == 0) & (qi == 0))
    def _():
        for b, blk in enumerate(_neg_cumsum_blocks(lft_ref[0])):
            negc_ref[b] = blk

    @pl.when(qi == 0)
    def _():
        kb_ref[...] = k_ref[0].astype(BF16)
        vb_ref[...] = v_ref[0].astype(BF16)

    qq = _stack_heads(q_ref[0])
    m_ref[...] = jnp.full(m_ref.shape, MASK_VALUE, F32)
    l_ref[...] = jnp.zeros(l_ref.shape, F32)
    acc_ref[...] = jnp.zeros(acc_ref.shape, F32)

    def chunk(j, masked):
        k0 = pl.multiple_of(j * tq, tq)
        s = _dot_nt(qq, kb_ref[pl.ds(k0, tq), :])
        nb0 = negc_ref[j, pl.ds(HEADS_PER_BLOCK * hp, 1), :]
        nb1 = negc_ref[j, pl.ds(HEADS_PER_BLOCK * hp + 1, 1), :]
        s = s + jnp.concatenate([jnp.broadcast_to(nb0, (tq, tq)), jnp.broadcast_to(nb1, (tq, tq))], axis=0)
        if masked:
            s = _causal_mask(s, tq)
        m_prev = m_ref[...]
        m_new = jnp.maximum(m_prev, jnp.max(s, axis=1, keepdims=True))
        alpha = jnp.exp(m_prev - m_new)
        p = jnp.exp(s - m_new)
        l_ref[...] = alpha * l_ref[...] + jnp.sum(p, axis=1, keepdims=True)
        acc_ref[...] = alpha * acc_ref[...] + _dot(p.astype(BF16), vb_ref[pl.ds(k0, tq), :])
        m_ref[...] = m_new

    def body(j, carry):
        chunk(j, False)
        return carry

    lax.fori_loop(0, qi, body, 0)
    chunk(qi, True)

    o = _unstack_heads(acc_ref[...] / l_ref[...], tq)
    o_ref[0] = (o * sga_ref[0].astype(F32)).astype(BF16)


def _attn_prompt(q, k, v, lft, sga):
    n_seq, seq_len, _ = q.shape
    tq = min(Q_TILE, seq_len)
    n_q = seq_len // tq
    qspec = pl.BlockSpec((1, tq, LANES), lambda b, hp, qi: (b, qi, hp))
    kvspec = pl.BlockSpec((1, seq_len, LANES), lambda b, hp, qi: (b, 0, hp))
    return pl.pallas_call(
        _attn_prompt_kernel,
        grid=(n_seq, N_HEAD_BLOCKS, n_q),
        in_specs=[qspec, kvspec, kvspec,
                  pl.BlockSpec((1, N_HEADS, seq_len), lambda b, hp, qi: (b, 0, 0)), qspec],
        out_specs=qspec,
        out_shape=jax.ShapeDtypeStruct(q.shape, BF16),
        scratch_shapes=[pltpu.VMEM((seq_len, LANES), BF16), pltpu.VMEM((seq_len, LANES), BF16),
                        pltpu.VMEM((seq_len // MXU_DIM, N_HEADS, MXU_DIM), F32),
                        pltpu.VMEM((2 * tq, 1), F32), pltpu.VMEM((2 * tq, 1), F32),
                        pltpu.VMEM((2 * tq, LANES), F32)],
        compiler_params=pltpu.CompilerParams(dimension_semantics=("arbitrary",) * 3,
                                             vmem_limit_bytes=VMEM_LIMIT_BYTES),
        name="attn_prompt",
    )(q, k, v, lft, sga)


def _attn_sample_kernel(q_ref, kp_ref, vp_ref, kn_ref, vn_ref, lft_ref, sga_ref, o_ref, negc_ref):
    hp = pl.program_id(1)
    tq = q_ref.shape[1]
    past = kp_ref.shape[1]

    @pl.when(hp == 0)
    def _():
        for b, blk in enumerate(_neg_cumsum_blocks(lft_ref[0])):
            negc_ref[:, b * MXU_DIM:(b + 1) * MXU_DIM] = blk

    qq = _stack_heads(q_ref[0])
    nb0 = negc_ref[pl.ds(HEADS_PER_BLOCK * hp, 1), :]
    nb1 = negc_ref[pl.ds(HEADS_PER_BLOCK * hp + 1, 1), :]

    def biased(s, lo, width):
        return s + jnp.concatenate([jnp.broadcast_to(nb0[:, lo:lo + width], (tq, width)),
                                    jnp.broadcast_to(nb1[:, lo:lo + width], (tq, width))], axis=0)

    s_past = biased(_dot_nt(qq, kp_ref[0].astype(BF16)), 0, past)
    s_new = _causal_mask(biased(_dot_nt(qq, kn_ref[0].astype(BF16)), past, tq), tq)
    m = jnp.maximum(jnp.max(s_past, axis=1, keepdims=True), jnp.max(s_new, axis=1, keepdims=True))
    p_past = jnp.exp(s_past - m)
    p_new = jnp.exp(s_new - m)
    l = jnp.sum(p_past, axis=1, keepdims=True) + jnp.sum(p_new, axis=1, keepdims=True)
    acc = _dot(p_past.astype(BF16), vp_ref[0].astype(BF16)) + _dot(p_new.astype(BF16), vn_ref[0].astype(BF16))
    o = _unstack_heads(acc / l, tq)
    o_ref[0] = (o * sga_ref[0].astype(F32)).astype(BF16)


def _attn_sample(q, k_past, v_past, k_new, v_new, lft_all, sga):
    n_seq, tq, _ = q.shape
    past = k_past.shape[1]
    total = lft_all.shape[2]
    qspec = pl.BlockSpec((1, tq, LANES), lambda b, hp: (b, 0, hp))
    pspec = pl.BlockSpec((1, past, LANES), lambda b, hp: (b, 0, hp))
    return pl.pallas_call(
        _attn_sample_kernel,
        grid=(n_seq, N_HEAD_BLOCKS),
        in_specs=[qspec, pspec, pspec, qspec, qspec,
                  pl.BlockSpec((1, N_HEADS, total), lambda b, hp: (b, 0, 0)), qspec],
        out_specs=qspec,
        out_shape=jax.ShapeDtypeStruct(q.shape, BF16),
        scratch_shapes=[pltpu.VMEM((N_HEADS, total), F32)],
        compiler_params=pltpu.CompilerParams(dimension_semantics=("arbitrary",) * 2,
                                             vmem_limit_bytes=VMEM_LIMIT_BYTES),
        name="attn_sample",
    )(q, k_past, v_past, k_new, v_new, lft_all, sga)


def _out_kernel(x_ref, oa_ref, u_ref, halo_ref, state_ref, sgb_ref, sma_ref, smb_ref,
                wdw_ref, bdw_ref, lng_ref, lnb_ref, wpa_ref, wpb_ref, wout_ref,
                y_ref, ext_ref, conv_ref, act_ref, *, seq_len):
    tm = x_ref.shape[0]
    row_chunk = min(LANES, tm)
    at_seq_start = (pl.program_id(0) * tm) % seq_len == 0

    halo = jnp.where(at_seq_start, state_ref[0], halo_ref[...].astype(F32))
    for c in range(N_LANE_CHUNKS):
        lanes = slice(c * LANES, (c + 1) * LANES)
        ext_ref[c, :HALO, :] = halo[:, lanes]
        ext_ref[c, HALO:, :] = u_ref[:, lanes].astype(F32)

    def conv_chunk(c, carry):
        for r0 in range(0, tm, row_chunk):
            acc = jnp.broadcast_to(bdw_ref[c], (row_chunk, LANES))
            for j in range(CONV_K):
                lo = r0 + HALO - CONV_HIST + j
                acc = acc + wdw_ref[c, j:j + 1, :] * ext_ref[c, lo:lo + row_chunk, :]
            conv_ref[c, r0:r0 + row_chunk, :] = acc
        return carry

    lax.fori_loop(0, N_LANE_CHUNKS, conv_chunk, 0)

    total = jnp.zeros((tm, 1), F32)
    for c in range(N_LANE_CHUNKS):
        total = total + jnp.sum(conv_ref[c], axis=-1, keepdims=True)
    mu = total * (1.0 / D_MODEL)
    sq = jnp.zeros((tm, 1), F32)
    for c in range(N_LANE_CHUNKS):
        xc = conv_ref[c] - mu
        sq = sq + jnp.sum(xc * xc, axis=-1, keepdims=True)
    rstd = lax.rsqrt(sq * (1.0 / D_MODEL) + EPS)
    for c in range(N_LANE_CHUNKS):
        lanes = slice(c * LANES, (c + 1) * LANES)
        y = (conv_ref[c] - mu) * rstd * lng_ref[:, lanes] + lnb_ref[:, lanes]
        act_ref[:, lanes] = (y * _sigmoid(y) * sgb_ref[:, lanes].astype(F32)).astype(BF16)

    ya = _dot(oa_ref[...], wpa_ref[...])
    yb = _dot(act_ref[...], wpb_ref[...])
    m = sma_ref[...].astype(F32) * ya + smb_ref[...].astype(F32) * yb
    y_ref[...] = x_ref[...] + _dot(m.astype(BF16), wout_ref[...])


def _out_proj(x2d, seq_len, oa, u, state, sgb, sma, smb, wdw, bdw, lng, lnb, wpa, wpb, wout):
    rows = x2d.shape[0]
    tm = min(ROW_TILE, seq_len)
    n_tiles = rows // tm
    halo_blocks = tm // HALO
    row_spec = pl.BlockSpec((tm, D_MODEL), lambda i: (i, 0))
    const = lambda shape: pl.BlockSpec(shape, lambda i: (0,) * len(shape))
    return pl.pallas_call(
        functools.partial(_out_kernel, seq_len=seq_len),
        grid=(n_tiles,),
        in_specs=[row_spec, row_spec, row_spec,
                  pl.BlockSpec((HALO, D_MODEL), lambda i: (jnp.maximum(i * halo_blocks - 1, 0), 0)),
                  pl.BlockSpec((1, HALO, D_MODEL), lambda i: (i * tm // seq_len, 0, 0)),
                  row_spec, row_spec, row_spec,
                  const((N_LANE_CHUNKS, HALO, LANES)), const((N_LANE_CHUNKS, 1, LANES)),
                  const((1, D_MODEL)), const((1, D_MODEL)),
                  const((D_MODEL, D_MODEL)), const((D_MODEL, D_MODEL)), const((D_MODEL, D_MODEL))],
        out_specs=row_spec,
        out_shape=jax.ShapeDtypeStruct((rows, D_MODEL), F32),
        scratch_shapes=[pltpu.VMEM((N_LANE_CHUNKS, HALO + tm, LANES), F32),
                        pltpu.VMEM((N_LANE_CHUNKS, tm, LANES), F32),
                        pltpu.VMEM((tm, D_MODEL), BF16)],
        compiler_params=pltpu.CompilerParams(dimension_semantics=("arbitrary",),
                                             vmem_limit_bytes=VMEM_LIMIT_BYTES),
        name="out_proj",
    )(x2d, oa, u, u, state, sgb, sma, smb, wdw, bdw, lng, lnb, wpa, wpb, wout)


def _lane_chunked(a):
    return a.reshape(a.shape[0], N_LANE_CHUNKS, LANES).transpose(1, 0, 2)


def _layer_params(norm_g, w_in, b_f, q_g, k_g, w_dw, b_dw, ln_g, ln_b, w_pa, w_pb, w_out):
    off_f = 3 * D_MODEL
    w_f = jnp.pad(w_in[:, off_f:off_f + N_HEADS], ((0, 0), (0, LANES - N_HEADS)))
    w_all = jnp.concatenate([w_in[:, :off_f], w_in[:, off_f + N_HEADS:], w_f], axis=1).astype(BF16)
    head_of_col = jnp.arange(D_MODEL) // HEAD_DIM
    e_mat = (head_of_col[:, None] == jnp.arange(LANES)[None, :]).astype(BF16)
    et_mat = jnp.concatenate([e_mat.T, e_mat.T], axis=0)
    return dict(
        ng=norm_g.reshape(1, D_MODEL), w_all=w_all,
        bf_pad=jnp.pad(b_f, (0, LANES - N_HEADS)).reshape(1, LANES),
        qg=(jnp.tile(q_g, N_HEADS) * (HEAD_DIM ** -0.5)).reshape(1, D_MODEL),
        kg=jnp.tile(k_g, N_HEADS).reshape(1, D_MODEL),
        e_mat=e_mat, et_mat=et_mat,
        wdw=_lane_chunked(jnp.pad(w_dw, ((0, HALO - CONV_K), (0, 0)))),
        bdw=_lane_chunked(b_dw.reshape(1, D_MODEL)),
        lng=ln_g.reshape(1, D_MODEL), lnb=ln_b.reshape(1, D_MODEL),
        wpa=w_pa.astype(BF16), wpb=w_pb.astype(BF16), wout=w_out.astype(BF16))


def _mixer_path(x, p, conv_state, cache=None):
    n_seq, seq_len, _ = x.shape
    x2d = x.reshape(n_seq * seq_len, D_MODEL)
    (q, k, v, lf, lft, sga, u, utail, sgb, sma, smb) = _inproj(
        x2d, seq_len, p["ng"], p["w_all"], p["bf_pad"], p["qg"], p["kg"], p["e_mat"], p["et_mat"])
    seq3 = lambda a: a.reshape(n_seq, seq_len, D_MODEL)
    if cache is None:
        oa = _attn_prompt(seq3(q), seq3(k), seq3(v), lft, seq3(sga))
    else:
        k_past, v_past, lf_past = cache
        past = k_past.shape[1]
        lft_new = lft.reshape(N_HEADS, n_seq, seq_len).transpose(1, 0, 2)
        total = past + seq_len
        pad = -total % MXU_DIM
        lft_all = jnp.pad(jnp.concatenate([lf_past.transpose(0, 2, 1), lft_new], axis=2),
                          ((0, 0), (0, 0), (0, pad)))
        oa = _attn_sample(seq3(q), k_past.reshape(n_seq, past, D_MODEL), v_past.reshape(n_seq, past, D_MODEL),
                          seq3(k), seq3(v), lft_all, seq3(sga))
    y = _out_proj(x2d, seq_len, oa.reshape(x2d.shape), u, conv_state, sgb, sma, smb,
                  p["wdw"], p["bdw"], p["lng"], p["lnb"], p["wpa"], p["wpb"], p["wout"])
    heads = lambda a: a.reshape(n_seq, seq_len, N_HEADS, HEAD_DIM)
    return (y.reshape(x.shape), heads(k), heads(v), lf.reshape(n_seq, seq_len, N_HEADS),
            utail[:, HALO - CONV_HIST:, :])


def kernel(x_prompt, x_sample, cache_k, cache_v, cache_logf, state_conv, norm_g, w_in, b_f, q_g, k_g,
           w_dw, b_dw, ln_g, ln_b, w_pa, w_pb, w_out):
    depth = w_in.shape[0]
    xp, xs = x_prompt, x_sample
    outs = [[] for _ in range(8)]
    for l in range(depth):
        p = _layer_params(norm_g[l], w_in[l], b_f[l], q_g[l], k_g[l], w_dw[l], b_dw[l], ln_g[l], ln_b[l],
                          w_pa[l], w_pb[l], w_out[l])
        fresh_state = jnp.zeros((xp.shape[0], HALO, D_MODEL), F32)
        xp, kp, vp, fp, cp = _mixer_path(xp, p, fresh_state)
        state = jnp.pad(state_conv[l].astype(F32), ((0, 0), (HALO - CONV_HIST, 0), (0, 0)))
        xs, ks, vs, fs, cs = _mixer_path(xs, p, state, cache=(cache_k[l], cache_v[l], cache_logf[l].astype(F32)))
        for lst, val in zip(outs, (kp, vp, fp, cp, ks, vs, fs, cs)):
            lst.append(val)
    return (xp, xs) + tuple(jnp.stack(o) for o in outs)
```

```python
import functools

import jax
import jax.numpy as jnp
from jax import lax
from jax.experimental import pallas as pl
from jax.experimental.pallas import tpu as pltpu

F32 = jnp.float32
BF16 = jnp.bfloat16

D_MODEL = 1024
N_HEADS = 16
HEAD_DIM = 64
CONV_K = 31
CONV_HIST = CONV_K - 1
EPS = 1e-6

LANES = 128
HALO = 32
N_LANE_CHUNKS = D_MODEL // LANES
HEADS_PER_BLOCK = LANES // HEAD_DIM
N_HEAD_BLOCKS = N_HEADS // HEADS_PER_BLOCK
MXU_DIM = 256
N_SECTIONS = 9
W_COLS = N_SECTIONS * D_MODEL + LANES
MASK_VALUE = -1e30
LOG2_E = 1.4426950408889634
SOFTMAX_ROWS = 64
KEY_CHUNKS_PER_UNIT = 2
VMEM_LIMIT_BYTES = 56 * 1024 * 1024

ROW_TILE = 512
Q_TILE = 256
ATTN_PAIRS_PER_STEP = 4
BF16_SUBLANES = 16


def _sigmoid(x):
    return 1.0 / (1.0 + jnp.exp(-x))


def _dot(a, b):
    return jnp.dot(a, b, preferred_element_type=F32)


def _dot_nt(a, b):
    return lax.dot_general(a, b, (((1,), (1,)), ((), ())), preferred_element_type=F32)


def _inproj_kernel(x_ref, ng_ref, w_ref, bf_ref, qg_ref, kg_ref, e_ref, et_ref,
                   q_ref, k_ref, v_ref, lf_ref, lft_ref, sga_ref, u_ref, utail_ref,
                   sgb_ref, sma_ref, smb_ref, *, seg):
    x = x_ref[...]
    ms = jnp.mean(x * x, axis=-1, keepdims=True)
    h = (x * lax.rsqrt(ms + EPS) * ng_ref[...]).astype(BF16)

    def proj(sec):
        return _dot(h, w_ref[:, sec * D_MODEL:(sec + 1) * D_MODEL])

    def head_rmsnorm(y, g):
        ss = _dot((y * y).astype(BF16), e_ref[...])
        inv = lax.rsqrt(ss * (1.0 / HEAD_DIM) + EPS)
        hi = inv.astype(BF16)
        lo = (inv - hi.astype(F32)).astype(BF16)
        scale = _dot(jnp.concatenate([hi, lo], axis=1), et_ref[...])
        return y * scale * g

    q_ref[...] = head_rmsnorm(proj(0), qg_ref[...]).astype(BF16)
    k_ref[...] = head_rmsnorm(proj(1), kg_ref[...])
    v_ref[...] = proj(2)

    z = _dot(h, w_ref[:, N_SECTIONS * D_MODEL:]) + bf_ref[...]
    lf = jnp.minimum(z, 0.0) - jnp.log1p(jnp.exp(-jnp.abs(z)))
    lf_ref[...] = lf[:, :N_HEADS]
    lft_ref[0] = lf.T[:N_HEADS, :]

    ga = proj(3)
    sga_ref[...] = (ga * _sigmoid(ga)).astype(BF16)
    u = proj(4) * _sigmoid(proj(5))
    u_ref[...] = u.astype(BF16)
    for s in range(utail_ref.shape[0]):
        utail_ref[s] = u[(s + 1) * seg - HALO:(s + 1) * seg]
    gb = proj(6)
    sgb_ref[...] = (gb * _sigmoid(gb)).astype(BF16)
    sma_ref[...] = _sigmoid(proj(7)).astype(BF16)
    smb_ref[...] = _sigmoid(proj(8)).astype(BF16)


def _inproj(x2d, seq_len, ng, w_all, bf_pad, qg, kg, e_mat, et_mat):
    rows = x2d.shape[0]
    tm = min(ROW_TILE, rows)
    n_tiles = rows // tm
    n_seq = rows // seq_len
    seg = min(seq_len, tm)
    seq_per_tile = tm // seg
    tiles_per_seq = seq_len // seg

    row_spec = pl.BlockSpec((tm, D_MODEL), lambda i: (i, 0))
    const = lambda shape: pl.BlockSpec(shape, lambda i: (0,) * len(shape))
    if tiles_per_seq > 1:
        lft_shape = (n_seq, N_HEADS, seq_len)
        lft_spec = pl.BlockSpec((1, N_HEADS, tm), lambda i: (i // tiles_per_seq, 0, i % tiles_per_seq))
    else:
        lft_shape = (n_tiles, N_HEADS, tm)
        lft_spec = pl.BlockSpec((1, N_HEADS, tm), lambda i: (i, 0, 0))
    utail_spec = pl.BlockSpec((seq_per_tile, HALO, D_MODEL),
                              lambda i: (i * seq_per_tile // tiles_per_seq, 0, 0))
    bf_rows = jax.ShapeDtypeStruct((rows, D_MODEL), BF16)
    f32_rows = jax.ShapeDtypeStruct((rows, D_MODEL), F32)
    return pl.pallas_call(
        functools.partial(_inproj_kernel, seg=seg),
        grid=(n_tiles,),
        in_specs=[row_spec, const((1, D_MODEL)),
                  pl.BlockSpec((D_MODEL, W_COLS), lambda i: (0, 0), pipeline_mode=pl.Buffered(1)),
                  const((1, LANES)), const((1, D_MODEL)), const((1, D_MODEL)),
                  const((D_MODEL, LANES)), const((2 * LANES, D_MODEL))],
        out_specs=[row_spec, row_spec, row_spec,
                   pl.BlockSpec((tm, N_HEADS), lambda i: (i, 0)), lft_spec,
                   row_spec, row_spec, utail_spec, row_spec, row_spec, row_spec],
        out_shape=[bf_rows, f32_rows, f32_rows,
                   jax.ShapeDtypeStruct((rows, N_HEADS), F32),
                   jax.ShapeDtypeStruct(lft_shape, F32),
                   bf_rows, bf_rows,
                   jax.ShapeDtypeStruct((n_seq, HALO, D_MODEL), F32),
                   bf_rows, bf_rows, bf_rows],
        compiler_params=pltpu.CompilerParams(dimension_semantics=("arbitrary",),
                                             vmem_limit_bytes=VMEM_LIMIT_BYTES),
        name="inproj",
    )(x2d, ng, w_all, bf_pad, qg, kg, e_mat, et_mat)


def _neg_cumsum_blocks(lf):
    hi = lf.astype(BF16)
    r1 = lf - hi.astype(F32)
    mid = r1.astype(BF16)
    lo = (r1 - mid.astype(F32)).astype(BF16)
    row = lax.broadcasted_iota(jnp.int32, (MXU_DIM, MXU_DIM), 0)
    col = lax.broadcasted_iota(jnp.int32, (MXU_DIM, MXU_DIM), 1)
    tri = (row <= col).astype(BF16)
    carry = jnp.zeros((lf.shape[0], 1), F32)
    out = []
    for b in range(lf.shape[1] // MXU_DIM):
        sl = slice(b * MXU_DIM, (b + 1) * MXU_DIM)
        c = _dot(hi[:, sl], tri) + _dot(mid[:, sl], tri) + _dot(lo[:, sl], tri) + carry
        out.append(-c * LOG2_E)
        carry = c[:, MXU_DIM - 1:MXU_DIM]
    return out


def _stack_heads(q):
    lane = lax.broadcasted_iota(jnp.int32, q.shape, 1)
    zero = jnp.zeros_like(q)
    return jnp.concatenate([jnp.where(lane < HEAD_DIM, q, zero),
                            jnp.where(lane >= HEAD_DIM, q, zero)], axis=0)


def _unstack_heads(o, tq):
    lane = lax.broadcasted_iota(jnp.int32, (tq, LANES), 1)
    return jnp.where(lane < HEAD_DIM, o[:tq], o[tq:])


def _causal_mask(s, tq):
    row = lax.broadcasted_iota(jnp.int32, s.shape, 0)
    row = jnp.where(row >= tq, row - tq, row)
    col = lax.broadcasted_iota(jnp.int32, s.shape, 1)
    return jnp.where(col <= row, s, MASK_VALUE)


def _bias_columns(lf):
    def split3(a):
        hi = a.astype(BF16)
        r1 = a - hi.astype(F32)
        mid = r1.astype(BF16)
        return hi, mid, (r1 - mid.astype(F32)).astype(BF16)

    seq_len = lf.shape[0]
    row = lax.broadcasted_iota(jnp.int32, (MXU_DIM, MXU_DIM), 0)
    col = lax.broadcasted_iota(jnp.int32, (MXU_DIM, MXU_DIM), 1)
    tril = (col <= row).astype(BF16)
    head = lax.broadcasted_iota(jnp.int32, (N_HEADS, LANES), 0)
    lane = lax.broadcasted_iota(jnp.int32, (N_HEADS, LANES), 1)
    place = [(lane == 3 * head + part).astype(BF16) for part in range(3)]
    parts = split3(lf)
    carry = jnp.zeros((1, N_HEADS), F32)
    out = []
    for b in range(seq_len // MXU_DIM):
        rows = slice(b * MXU_DIM, (b + 1) * MXU_DIM)
        c = carry
        for part in parts:
            c = c + _dot(tril, part[rows])
        carry = c[MXU_DIM - 1:MXU_DIM]
        out.append(sum(_dot(t, sel) for t, sel in zip(split3(-c * LOG2_E), place)).astype(BF16))
    return out


def _attn_prompt_kernel(q_ref, k_ref, v_ref, lf_ref, sga_ref, o_ref,
                        kb_ref, vt_ref, aug_ref, qt_ref, m_ref, acc_ref, st_ref, p_ref, alpha_ref):
    grp = pl.program_id(1)
    qi = pl.program_id(2)
    tq = q_ref.shape[1]
    tk = MXU_DIM
    n_pairs = kb_ref.shape[0]
    n_heads = n_pairs * HEADS_PER_BLOCK

    @pl.when((grp == 0) & (qi == 0))
    def _():
        for b, blk in enumerate(_bias_columns(lf_ref[0])):
            aug_ref[b * MXU_DIM:(b + 1) * MXU_DIM, :] = blk

    @pl.when(qi == 0)
    def _():
        ones = jnp.ones((vt_ref.shape[2] - HEAD_DIM, tk), BF16)
        for pair in range(n_pairs):
            lanes = slice(pair * LANES, (pair + 1) * LANES)
            kb_ref[pair] = k_ref[0, :, lanes].astype(BF16)
            for c in range(vt_ref.shape[1]):
                vt = v_ref[0, c * tk:(c + 1) * tk, lanes].T.astype(BF16)
                for half in range(HEADS_PER_BLOCK):
                    h = pair * HEADS_PER_BLOCK + half
                    vt_ref[h, c, :HEAD_DIM, :] = vt[half * HEAD_DIM:(half + 1) * HEAD_DIM]
                    vt_ref[h, c, HEAD_DIM:, :] = ones

    row = lax.broadcasted_iota(jnp.int32, (LANES, tq), 0)
    for pair in range(n_pairs):
        qt = q_ref[0, :, pair * LANES:(pair + 1) * LANES].astype(F32).T
        for half in range(HEADS_PER_BLOCK):
            h = pair * HEADS_PER_BLOCK + half
            own = (row >= half * HEAD_DIM) & (row < (half + 1) * HEAD_DIM)
            qt_ref[h, :LANES, :] = jnp.where(own, qt, 0.0).astype(BF16)
            lo = 3 * (grp * n_heads + h)
            qt_ref[h, LANES:, :] = ((row >= lo) & (row < lo + 3)).astype(BF16)
    m_ref[...] = jnp.full(m_ref.shape, MASK_VALUE, F32)
    acc_ref[...] = jnp.zeros(acc_ref.shape, F32)

    def scores(h, j, n_sub):
        rows = n_sub * tk
        k0 = pl.multiple_of(j * tk, tk)
        keys = jnp.concatenate([kb_ref[h // HEADS_PER_BLOCK, pl.ds(k0, rows), :], aug_ref[pl.ds(k0, rows), :]],
                               axis=1)
        st_ref[h, :rows, :] = _dot(keys, qt_ref[h])

    def softmax(h, n_sub, mask_last):
        rows = n_sub * tk
        diag0 = (n_sub - 1) * tk
        for c in range(tq // LANES):
            cols = slice(c * LANES, (c + 1) * LANES)

            def block(r):
                st = st_ref[h, r:r + SOFTMAX_ROWS, cols]
                if mask_last and r >= diag0:
                    key = lax.broadcasted_iota(jnp.int32, st.shape, 0) + (r - diag0)
                    qry = lax.broadcasted_iota(jnp.int32, st.shape, 1) + c * LANES
                    st = jnp.where(key <= qry, st, MASK_VALUE)
                return st

            m8 = None
            for r in range(0, rows, SOFTMAX_ROWS):
                bm = jnp.max(block(r).reshape(SOFTMAX_ROWS // 8, 8, LANES), axis=0)
                m8 = bm if m8 is None else jnp.maximum(m8, bm)
            m_prev = m_ref[h, :, cols]
            m_new = jnp.maximum(m_prev, jnp.max(m8, axis=0, keepdims=True))
            alpha_ref[h, :, cols] = jnp.exp2(m_prev - m_new)
            m_ref[h, :, cols] = m_new
            for r in range(0, rows, SOFTMAX_ROWS):
                p_ref[h, r:r + SOFTMAX_ROWS, cols] = jnp.exp2(block(r) - m_new).astype(BF16)

    def values(h, j, n_sub):
        pv = _dot(vt_ref[h, j], p_ref[h, :tk, :])
        for s in range(1, n_sub):
            pv = pv + _dot(vt_ref[h, j + s], p_ref[h, s * tk:(s + 1) * tk, :])
        acc_ref[h] = alpha_ref[h] * acc_ref[h] + pv

    def unit(j, n_sub, mask_last):
        for h in range(n_heads):
            scores(h, j, n_sub)
        for h in range(n_heads):
            softmax(h, n_sub, mask_last)
        for h in range(n_heads):
            values(h, j, n_sub)

    def body(jj, carry):
        unit(jj * KEY_CHUNKS_PER_UNIT, KEY_CHUNKS_PER_UNIT, False)
        return carry

    n_full = (qi + 1) // KEY_CHUNKS_PER_UNIT
    n_tail = (qi + 1) % KEY_CHUNKS_PER_UNIT
    lax.fori_loop(0, n_full - (n_tail == 0).astype(jnp.int32), body, 0)
    for tail in range(KEY_CHUNKS_PER_UNIT):
        n_sub = tail if tail else KEY_CHUNKS_PER_UNIT

        @pl.when(n_tail == tail)
        def _():
            unit(qi + 1 - n_sub, n_sub, True)

    for pair in range(n_pairs):
        lanes = slice(pair * LANES, (pair + 1) * LANES)
        ot = []
        for half in range(HEADS_PER_BLOCK):
            acc = acc_ref[pair * HEADS_PER_BLOCK + half]
            ot.append(acc[:HEAD_DIM] / acc[HEAD_DIM:HEAD_DIM + 1])
        o = jnp.concatenate(ot, axis=0).T
        o_ref[0, :, lanes] = (o * sga_ref[0, :, lanes].astype(F32)).astype(BF16)


def _attn_prompt(q, k, v, lf, sga):
    n_seq, seq_len, _ = q.shape
    tq = Q_TILE
    assert tq == MXU_DIM and seq_len % tq == 0
    n_q = seq_len // tq
    n_pairs = ATTN_PAIRS_PER_STEP
    n_heads = n_pairs * HEADS_PER_BLOCK
    width = n_pairs * LANES
    acc_rows = HEAD_DIM + BF16_SUBLANES
    qspec = pl.BlockSpec((1, tq, width), lambda b, g, qi: (b, qi, g))
    kvspec = pl.BlockSpec((1, seq_len, width), lambda b, g, qi: (b, 0, g))
    return pl.pallas_call(
        _attn_prompt_kernel,
        grid=(n_seq, D_MODEL // width, n_q),
        in_specs=[qspec, kvspec, kvspec,
                  pl.BlockSpec((1, seq_len, N_HEADS), lambda b, g, qi: (b, 0, 0)), qspec],
        out_specs=qspec,
        out_shape=jax.ShapeDtypeStruct(q.shape, BF16),
        scratch_shapes=[pltpu.VMEM((n_pairs, seq_len, LANES), BF16),
                        pltpu.VMEM((n_heads, n_q, acc_rows, tq), BF16),
                        pltpu.VMEM((seq_len, LANES), BF16),
                        pltpu.VMEM((n_heads, 2 * LANES, tq), BF16),
                        pltpu.VMEM((n_heads, 1, tq), F32),
                        pltpu.VMEM((n_heads, acc_rows, tq), F32),
                        pltpu.VMEM((n_heads, KEY_CHUNKS_PER_UNIT * MXU_DIM, tq), F32),
                        pltpu.VMEM((n_heads, KEY_CHUNKS_PER_UNIT * MXU_DIM, tq), BF16),
                        pltpu.VMEM((n_heads, 1, tq), F32)],
        compiler_params=pltpu.CompilerParams(dimension_semantics=("arbitrary",) * 3,
                                             vmem_limit_bytes=VMEM_LIMIT_BYTES),
        name="attn_prompt",
    )(q, k, v, lf, sga)


def _attn_sample_kernel(q_ref, kp_ref, vp_ref, kn_ref, vn_ref, lft_ref, sga_ref, o_ref, negc_ref):
    hp = pl.program_id(1)
    tq = q_ref.shape[1]
    past = kp_ref.shape[1]

    @pl.when(hp == 0)
    def _():
        for b, blk in enumerate(_neg_cumsum_blocks(lft_ref[0])):
            negc_ref[:, b * MXU_DIM:(b + 1) * MXU_DIM] = blk

    qq = _stack_heads(q_ref[0])
    nb0 = negc_ref[pl.ds(HEADS_PER_BLOCK * hp, 1), :]
    nb1 = negc_ref[pl.ds(HEADS_PER_BLOCK * hp + 1, 1), :]

    def biased(s, lo, width):
        return s + jnp.concatenate([jnp.broadcast_to(nb0[:, lo:lo + width], (tq, width)),
                                    jnp.broadcast_to(nb1[:, lo:lo + width], (tq, width))], axis=0)

    s_past = biased(_dot_nt(qq, kp_ref[0].astype(BF16)), 0, past)
    s_new = _causal_mask(biased(_dot_nt(qq, kn_ref[0].astype(BF16)), past, tq), tq)
    m = jnp.maximum(jnp.max(s_past, axis=1, keepdims=True), jnp.max(s_new, axis=1, keepdims=True))
    p_past = jnp.exp2(s_past - m)
    p_new = jnp.exp2(s_new - m)
    l = jnp.sum(p_past, axis=1, keepdims=True) + jnp.sum(p_new, axis=1, keepdims=True)
    acc = _dot(p_past.astype(BF16), vp_ref[0].astype(BF16)) + _dot(p_new.astype(BF16), vn_ref[0].astype(BF16))
    o = _unstack_heads(acc / l, tq)
    o_ref[0] = (o * sga_ref[0].astype(F32)).astype(BF16)


def _attn_sample(q, k_past, v_past, k_new, v_new, lft_all, sga):
    n_seq, tq, _ = q.shape
    past = k_past.shape[1]
    total = lft_all.shape[2]
    qspec = pl.BlockSpec((1, tq, LANES), lambda b, hp: (b, 0, hp))
    pspec = pl.BlockSpec((1, past, LANES), lambda b, hp: (b, 0, hp))
    return pl.pallas_call(
        _attn_sample_kernel,
        grid=(n_seq, N_HEAD_BLOCKS),
        in_specs=[qspec, pspec, pspec, qspec, qspec,
                  pl.BlockSpec((1, N_HEADS, total), lambda b, hp: (b, 0, 0)), qspec],
        out_specs=qspec,
        out_shape=jax.ShapeDtypeStruct(q.shape, BF16),
        scratch_shapes=[pltpu.VMEM((N_HEADS, total), F32)],
        compiler_params=pltpu.CompilerParams(dimension_semantics=("arbitrary",) * 2,
                                             vmem_limit_bytes=VMEM_LIMIT_BYTES),
        name="attn_sample",
    )(q, k_past, v_past, k_new, v_new, lft_all, sga)


def _out_kernel(x_ref, oa_ref, u_ref, halo_ref, state_ref, sgb_ref, sma_ref, smb_ref,
                wdw_ref, bdw_ref, lng_ref, lnb_ref, wpa_ref, wpb_ref, wout_ref,
                y_ref, ext_ref, conv_ref, act_ref, *, seq_len):
    tm = x_ref.shape[0]
    row_chunk = min(LANES, tm)
    at_seq_start = (pl.program_id(0) * tm) % seq_len == 0

    halo = jnp.where(at_seq_start, state_ref[0], halo_ref[...].astype(F32))
    for c in range(N_LANE_CHUNKS):
        lanes = slice(c * LANES, (c + 1) * LANES)
        ext_ref[c, :HALO, :] = halo[:, lanes]
        ext_ref[c, HALO:, :] = u_ref[:, lanes].astype(F32)

    def conv_chunk(c, carry):
        for r0 in range(0, tm, row_chunk):
            acc = jnp.broadcast_to(bdw_ref[c], (row_chunk, LANES))
            for j in range(CONV_K):
                lo = r0 + HALO - CONV_HIST + j
                acc = acc + wdw_ref[c, j:j + 1, :] * ext_ref[c, lo:lo + row_chunk, :]
            conv_ref[c, r0:r0 + row_chunk, :] = acc
        return carry

    lax.fori_loop(0, N_LANE_CHUNKS, conv_chunk, 0)

    total = jnp.zeros((tm, 1), F32)
    for c in range(N_LANE_CHUNKS):
        total = total + jnp.sum(conv_ref[c], axis=-1, keepdims=True)
    mu = total * (1.0 / D_MODEL)
    sq = jnp.zeros((tm, 1), F32)
    for c in range(N_LANE_CHUNKS):
        xc = conv_ref[c] - mu
        sq = sq + jnp.sum(xc * xc, axis=-1, keepdims=True)
    rstd = lax.rsqrt(sq * (1.0 / D_MODEL) + EPS)
    for c in range(N_LANE_CHUNKS):
        lanes = slice(c * LANES, (c + 1) * LANES)
        y = (conv_ref[c] - mu) * rstd * lng_ref[:, lanes] + lnb_ref[:, lanes]
        act_ref[:, lanes] = (y * _sigmoid(y) * sgb_ref[:, lanes].astype(F32)).astype(BF16)

    ya = _dot(oa_ref[...], wpa_ref[...])
    yb = _dot(act_ref[...], wpb_ref[...])
    m = sma_ref[...].astype(F32) * ya + smb_ref[...].astype(F32) * yb
    y_ref[...] = x_ref[...] + _dot(m.astype(BF16), wout_ref[...])


def _out_proj(x2d, seq_len, oa, u, state, sgb, sma, smb, wdw, bdw, lng, lnb, wpa, wpb, wout):
    rows = x2d.shape[0]
    tm = min(ROW_TILE, seq_len)
    n_tiles = rows // tm
    halo_blocks = tm // HALO
    row_spec = pl.BlockSpec((tm, D_MODEL), lambda i: (i, 0))
    const = lambda shape: pl.BlockSpec(shape, lambda i: (0,) * len(shape))
    return pl.pallas_call(
        functools.partial(_out_kernel, seq_len=seq_len),
        grid=(n_tiles,),
        in_specs=[row_spec, row_spec, row_spec,
                  pl.BlockSpec((HALO, D_MODEL), lambda i: (jnp.maximum(i * halo_blocks - 1, 0), 0)),
                  pl.BlockSpec((1, HALO, D_MODEL), lambda i: (i * tm // seq_len, 0, 0)),
                  row_spec, row_spec, row_spec,
                  const((N_LANE_CHUNKS, HALO, LANES)), const((N_LANE_CHUNKS, 1, LANES)),
                  const((1, D_MODEL)), const((1, D_MODEL)),
                  const((D_MODEL, D_MODEL)), const((D_MODEL, D_MODEL)), const((D_MODEL, D_MODEL))],
        out_specs=row_spec,
        out_shape=jax.ShapeDtypeStruct((rows, D_MODEL), F32),
        scratch_shapes=[pltpu.VMEM((N_LANE_CHUNKS, HALO + tm, LANES), F32),
                        pltpu.VMEM((N_LANE_CHUNKS, tm, LANES), F32),
                        pltpu.VMEM((tm, D_MODEL), BF16)],
        compiler_params=pltpu.CompilerParams(dimension_semantics=("arbitrary",),
                                             vmem_limit_bytes=VMEM_LIMIT_BYTES),
        name="out_proj",
    )(x2d, oa, u, u, state, sgb, sma, smb, wdw, bdw, lng, lnb, wpa, wpb, wout)


def _lane_chunked(a):
    return a.reshape(a.shape[0], N_LANE_CHUNKS, LANES).transpose(1, 0, 2)


def _layer_params(norm_g, w_in, b_f, q_g, k_g, w_dw, b_dw, ln_g, ln_b, w_pa, w_pb, w_out):
    off_f = 3 * D_MODEL
    w_f = jnp.pad(w_in[:, off_f:off_f + N_HEADS], ((0, 0), (0, LANES - N_HEADS)))
    w_all = jnp.concatenate([w_in[:, :off_f], w_in[:, off_f + N_HEADS:], w_f], axis=1).astype(BF16)
    head_of_col = jnp.arange(D_MODEL) // HEAD_DIM
    e_mat = (head_of_col[:, None] == jnp.arange(LANES)[None, :]).astype(BF16)
    et_mat = jnp.concatenate([e_mat.T, e_mat.T], axis=0)
    return dict(
        ng=norm_g.reshape(1, D_MODEL), w_all=w_all,
        bf_pad=jnp.pad(b_f, (0, LANES - N_HEADS)).reshape(1, LANES),
        qg=(jnp.tile(q_g, N_HEADS) * (HEAD_DIM ** -0.5 * LOG2_E)).reshape(1, D_MODEL),
        kg=jnp.tile(k_g, N_HEADS).reshape(1, D_MODEL),
        e_mat=e_mat, et_mat=et_mat,
        wdw=_lane_chunked(jnp.pad(w_dw, ((0, HALO - CONV_K), (0, 0)))),
        bdw=_lane_chunked(b_dw.reshape(1, D_MODEL)),
        lng=ln_g.reshape(1, D_MODEL), lnb=ln_b.reshape(1, D_MODEL),
        wpa=w_pa.astype(BF16), wpb=w_pb.astype(BF16), wout=w_out.astype(BF16))


def _mixer_path(x, p, conv_state, cache=None):
    n_seq, seq_len, _ = x.shape
    x2d = x.reshape(n_seq * seq_len, D_MODEL)
    (q, k, v, lf, lft, sga, u, utail, sgb, sma, smb) = _inproj(
        x2d, seq_len, p["ng"], p["w_all"], p["bf_pad"], p["qg"], p["kg"], p["e_mat"], p["et_mat"])
    seq3 = lambda a: a.reshape(n_seq, seq_len, D_MODEL)
    if cache is None:
        oa = _attn_prompt(seq3(q), seq3(k), seq3(v), lf.reshape(n_seq, seq_len, N_HEADS), seq3(sga))
    else:
        k_past, v_past, lf_past = cache
        past = k_past.shape[1]
        lft_new = lft.reshape(N_HEADS, n_seq, seq_len).transpose(1, 0, 2)
        total = past + seq_len
        pad = -total % MXU_DIM
        lft_all = jnp.pad(jnp.concatenate([lf_past.transpose(0, 2, 1), lft_new], axis=2),
                          ((0, 0), (0, 0), (0, pad)))
        oa = _attn_sample(seq3(q), k_past.reshape(n_seq, past, D_MODEL), v_past.reshape(n_seq, past, D_MODEL),
                          seq3(k), seq3(v), lft_all, seq3(sga))
    y = _out_proj(x2d, seq_len, oa.reshape(x2d.shape), u, conv_state, sgb, sma, smb,
                  p["wdw"], p["bdw"], p["lng"], p["lnb"], p["wpa"], p["wpb"], p["wout"])
    heads = lambda a: a.reshape(n_seq, seq_len, N_HEADS, HEAD_DIM)
    return (y.reshape(x.shape), heads(k), heads(v), lf.reshape(n_seq, seq_len, N_HEADS),
            utail[:, HALO - CONV_HIST:, :])


def kernel(x_prompt, x_sample, cache_k, cache_v, cache_logf, state_conv, norm_g, w_in, b_f, q_g, k_g,
           w_dw, b_dw, ln_g, ln_b, w_pa, w_pb, w_out):
    depth = w_in.shape[0]
    xp, xs = x_prompt, x_sample
    outs = [[] for _ in range(8)]
    for l in range(depth):
        p = _layer_params(norm_g[l], w_in[l], b_f[l], q_g[l], k_g[l], w_dw[l], b_dw[l], ln_g[l], ln_b[l],
                          w_pa[l], w_pb[l], w_out[l])
        fresh_state = jnp.zeros((xp.shape[0], HALO, D_MODEL), F32)
        xp, kp, vp, fp, cp = _mixer_path(xp, p, fresh_state)
        state = jnp.pad(state_conv[l].astype(F32), ((0, 0), (HALO - CONV_HIST, 0), (0, 0)))
        xs, ks, vs, fs, cs = _mixer_path(xs, p, state, cache=(cache_k[l], cache_v[l], cache_logf[l].astype(F32)))
        for lst, val in zip(outs, (kp, vp, fp, cp, ks, vs, fs, cs)):
            lst.append(val)
    return (xp, xs) + tuple(jnp.stack(o) for o in outs)
```

```python
import functools

import jax
import jax.numpy as jnp
from jax import lax
from jax.experimental import pallas as pl
from jax.experimental.pallas import tpu as pltpu

F32 = jnp.float32
BF16 = jnp.bfloat16

D_MODEL = 1024
N_HEADS = 16
HEAD_DIM = 64
CONV_K = 31
CONV_HIST = CONV_K - 1
EPS = 1e-6

LANES = 128
BF16_SUBLANES = 16
HALO = 32
N_LANE_CHUNKS = D_MODEL // LANES
HEADS_PER_BLOCK = LANES // HEAD_DIM
N_HEAD_BLOCKS = N_HEADS // HEADS_PER_BLOCK
MXU_DIM = 256
SEC_Q, SEC_K, SEC_V, SEC_GA, SEC_UA, SEC_UB, SEC_GB, SEC_MA, SEC_MB = range(9)
N_SECTIONS = 9
W_COLS = N_SECTIONS * D_MODEL + LANES
MASK_VALUE = -1e30
LOG2_E = 1.4426950408889634
VMEM_LIMIT_BYTES = 56 * 1024 * 1024

INPROJ_ROW_TILE = 256
OUT_ROW_TILE = 512
Q_TILE = 256
ATTN_PAIRS_PER_STEP = 4
SOFTMAX_ROWS = 64
KEY_CHUNKS_PER_UNIT = 2


def _sigmoid(x):
    return 1.0 / (1.0 + jnp.exp(-x))


def _dot(a, b):
    return jnp.dot(a, b, preferred_element_type=F32)


def _dot_nt(a, b):
    return lax.dot_general(a, b, (((1,), (1,)), ((), ())), preferred_element_type=F32)


def _lane_chunk(c):
    return slice(c * LANES, (c + 1) * LANES)


def _inproj_kernel(x_ref, ng_ref, w_ref, bf_ref, qg_ref, kg_ref, e_ref, et_ref,
                   state_ref, wdw_ref, bdw_ref, lng_ref, lnb_ref,
                   q_ref, k_ref, v_ref, kb_ref, vb_ref, lf_ref, lft_ref, sga_ref, cg_ref, utail_ref,
                   sma_ref, smb_ref, ext_ref, conv_ref, *, seg, tiles_per_seq, time_minor):
    tm = x_ref.shape[0]
    n_seg = tm // seg
    stride = HALO + seg

    x = x_ref[...]
    ms = jnp.mean(x * x, axis=-1, keepdims=True)
    h = (x * lax.rsqrt(ms + EPS) * ng_ref[...]).astype(BF16)

    def proj(sec):
        return _dot(h, w_ref[:, sec * D_MODEL:(sec + 1) * D_MODEL])

    u = proj(SEC_UA) * _sigmoid(proj(SEC_UB))
    for s in range(n_seg):
        utail_ref[s] = u[(s + 1) * seg - HALO:(s + 1) * seg]
    if tiles_per_seq > 1:
        @pl.when(pl.program_id(0) % tiles_per_seq == 0)
        def _():
            for c in range(N_LANE_CHUNKS):
                ext_ref[c, :HALO, :] = state_ref[0, :, _lane_chunk(c)]
    else:
        for s in range(n_seg):
            for c in range(N_LANE_CHUNKS):
                ext_ref[c, s * stride:s * stride + HALO, :] = state_ref[s, :, _lane_chunk(c)]
    for s in range(n_seg):
        for c in range(N_LANE_CHUNKS):
            ext_ref[c, s * stride + HALO:(s + 1) * stride, :] = u[s * seg:(s + 1) * seg, _lane_chunk(c)]

    row_chunk = min(LANES, seg)

    def conv(c):
        for s in range(n_seg):
            for r0 in range(0, seg, row_chunk):
                base = s * stride + r0 + HALO - CONV_HIST
                acc = jnp.broadcast_to(bdw_ref[c], (row_chunk, LANES))
                for j in range(CONV_K):
                    acc = acc + wdw_ref[c, j:j + 1, :] * ext_ref[c, base + j:base + j + row_chunk, :]
                conv_ref[c, s * seg + r0:s * seg + r0 + row_chunk, :] = acc
        if tiles_per_seq > 1:
            ext_ref[c, :HALO, :] = u[tm - HALO:, _lane_chunk(c)]

    def head_rmsnorm(y, g):
        ss = _dot((y * y).astype(BF16), e_ref[...])
        inv = lax.rsqrt(ss * (1.0 / HEAD_DIM) + EPS)
        hi = inv.astype(BF16)
        lo = (inv - hi.astype(F32)).astype(BF16)
        scale = _dot(jnp.concatenate([hi, lo], axis=1), et_ref[...])
        return y * scale * g

    def store_state(value, state_out_ref):
        if time_minor:
            state_out_ref[0] = value.T
        else:
            for hd in range(N_HEADS):
                state_out_ref[:, hd, :] = value[:, hd * HEAD_DIM:(hd + 1) * HEAD_DIM]

    conv(0)
    q_ref[...] = head_rmsnorm(proj(SEC_Q), qg_ref[...]).astype(BF16)
    conv(1)
    k = head_rmsnorm(proj(SEC_K), kg_ref[...])
    store_state(k, k_ref)
    kb_ref[...] = k.astype(BF16)
    conv(2)
    v = proj(SEC_V)
    store_state(v, v_ref)
    if time_minor:
        vb_ref[0] = v.T.astype(BF16)
    else:
        vb_ref[...] = v.astype(BF16)
    conv(3)

    z = _dot(h, w_ref[:, N_SECTIONS * D_MODEL:]) + bf_ref[...]
    lf = jnp.minimum(z, 0.0) - jnp.log1p(jnp.exp(-jnp.abs(z)))
    lf_ref[...] = lf[:, :N_HEADS]
    lft_ref[0] = lf.T[:N_HEADS, :]

    ga = proj(SEC_GA)
    sga_ref[...] = (ga * _sigmoid(ga)).astype(BF16)
    conv(4)
    sma_ref[...] = _sigmoid(proj(SEC_MA)).astype(BF16)
    conv(5)
    smb_ref[...] = _sigmoid(proj(SEC_MB)).astype(BF16)
    conv(6)

    gb = proj(SEC_GB)
    sgb = gb * _sigmoid(gb)
    conv(7)
    total = jnp.zeros((tm, 1), F32)
    for c in range(N_LANE_CHUNKS):
        total = total + jnp.sum(conv_ref[c], axis=-1, keepdims=True)
    mu = total * (1.0 / D_MODEL)
    sq = jnp.zeros((tm, 1), F32)
    for c in range(N_LANE_CHUNKS):
        xc = conv_ref[c] - mu
        sq = sq + jnp.sum(xc * xc, axis=-1, keepdims=True)
    rstd = lax.rsqrt(sq * (1.0 / D_MODEL) + EPS)
    for c in range(N_LANE_CHUNKS):
        lanes = _lane_chunk(c)
        y = (conv_ref[c] - mu) * rstd * lng_ref[:, lanes] + lnb_ref[:, lanes]
        cg_ref[:, lanes] = (y * _sigmoid(y) * sgb[:, lanes]).astype(BF16)


def _inproj(x2d, seq_len, conv_state, p, time_minor):
    rows = x2d.shape[0]
    tm = min(INPROJ_ROW_TILE, rows)
    n_tiles = rows // tm
    n_seq = rows // seq_len
    seg = min(seq_len, tm)
    seq_per_tile = tm // seg
    tiles_per_seq = seq_len // seg

    row_spec = pl.BlockSpec((tm, D_MODEL), lambda i: (i, 0))
    assert seq_per_tile == 1 or tiles_per_seq == 1
    seq_spec = pl.BlockSpec((seq_per_tile, HALO, D_MODEL), lambda i: (i // tiles_per_seq, 0, 0))
    const = lambda shape: pl.BlockSpec(shape, lambda i: (0,) * len(shape))
    bf_rows = jax.ShapeDtypeStruct((rows, D_MODEL), BF16)
    if time_minor:
        assert seq_per_tile == 1
        time_spec = lambda ch: pl.BlockSpec((1, ch, tm), lambda i: (i // tiles_per_seq, 0, i % tiles_per_seq))
        state_spec, vb_spec, lft_spec = time_spec(D_MODEL), time_spec(D_MODEL), time_spec(N_HEADS)
        state_shape = jax.ShapeDtypeStruct((n_seq, D_MODEL, seq_len), F32)
        vb_shape = jax.ShapeDtypeStruct((n_seq, D_MODEL, seq_len), BF16)
        lft_shape = jax.ShapeDtypeStruct((n_seq, N_HEADS, seq_len), F32)
    else:
        state_spec = pl.BlockSpec((tm, N_HEADS, HEAD_DIM), lambda i: (i, 0, 0))
        vb_spec = row_spec
        lft_spec = pl.BlockSpec((1, N_HEADS, tm), lambda i: (i, 0, 0))
        state_shape = jax.ShapeDtypeStruct((rows, N_HEADS, HEAD_DIM), F32)
        vb_shape = bf_rows
        lft_shape = jax.ShapeDtypeStruct((n_tiles, N_HEADS, tm), F32)
    return pl.pallas_call(
        functools.partial(_inproj_kernel, seg=seg, tiles_per_seq=tiles_per_seq, time_minor=time_minor),
        grid=(n_tiles,),
        in_specs=[row_spec, const((1, D_MODEL)),
                  pl.BlockSpec((D_MODEL, W_COLS), lambda i: (0, 0), pipeline_mode=pl.Buffered(1)),
                  const((1, LANES)), const((1, D_MODEL)), const((1, D_MODEL)),
                  const((D_MODEL, LANES)), const((2 * LANES, D_MODEL)),
                  seq_spec, const((N_LANE_CHUNKS, HALO, LANES)), const((N_LANE_CHUNKS, 1, LANES)),
                  const((1, D_MODEL)), const((1, D_MODEL))],
        out_specs=[row_spec, state_spec, state_spec, row_spec, vb_spec,
                   pl.BlockSpec((tm, N_HEADS), lambda i: (i, 0)), lft_spec,
                   row_spec, row_spec, seq_spec, row_spec, row_spec],
        out_shape=[bf_rows, state_shape, state_shape, bf_rows, vb_shape,
                   jax.ShapeDtypeStruct((rows, N_HEADS), F32), lft_shape,
                   bf_rows, bf_rows,
                   jax.ShapeDtypeStruct((n_seq, HALO, D_MODEL), F32),
                   bf_rows, bf_rows],
        scratch_shapes=[pltpu.VMEM((N_LANE_CHUNKS, seq_per_tile * (HALO + seg), LANES), F32),
                        pltpu.VMEM((N_LANE_CHUNKS, tm, LANES), F32)],
        compiler_params=pltpu.CompilerParams(dimension_semantics=("arbitrary",),
                                             vmem_limit_bytes=VMEM_LIMIT_BYTES),
        name="inproj",
    )(x2d, p["ng"], p["w_all"], p["bf_pad"], p["qg"], p["kg"], p["e_mat"], p["et_mat"],
      conv_state, p["wdw"], p["bdw"], p["lng"], p["lnb"])


def _neg_cumsum_blocks(lf):
    hi = lf.astype(BF16)
    r1 = lf - hi.astype(F32)
    mid = r1.astype(BF16)
    lo = (r1 - mid.astype(F32)).astype(BF16)
    row = lax.broadcasted_iota(jnp.int32, (MXU_DIM, MXU_DIM), 0)
    col = lax.broadcasted_iota(jnp.int32, (MXU_DIM, MXU_DIM), 1)
    tri = (row <= col).astype(BF16)
    carry = jnp.zeros((lf.shape[0], 1), F32)
    out = []
    for b in range(lf.shape[1] // MXU_DIM):
        sl = slice(b * MXU_DIM, (b + 1) * MXU_DIM)
        c = _dot(hi[:, sl], tri) + _dot(mid[:, sl], tri) + _dot(lo[:, sl], tri) + carry
        out.append(-c * LOG2_E)
        carry = c[:, MXU_DIM - 1:MXU_DIM]
    return out


def _stack_heads(q):
    lane = lax.broadcasted_iota(jnp.int32, q.shape, 1)
    zero = jnp.zeros_like(q)
    return jnp.concatenate([jnp.where(lane < HEAD_DIM, q, zero),
                            jnp.where(lane >= HEAD_DIM, q, zero)], axis=0)


def _unstack_heads(o, tq):
    lane = lax.broadcasted_iota(jnp.int32, (tq, LANES), 1)
    return jnp.where(lane < HEAD_DIM, o[:tq], o[tq:])


def _causal_mask(s, tq):
    row = lax.broadcasted_iota(jnp.int32, s.shape, 0)
    row = jnp.where(row >= tq, row - tq, row)
    col = lax.broadcasted_iota(jnp.int32, s.shape, 1)
    return jnp.where(col <= row, s, MASK_VALUE)


def _bias_columns(lf):
    def split3(a):
        hi = a.astype(BF16)
        r1 = a - hi.astype(F32)
        mid = r1.astype(BF16)
        return hi, mid, (r1 - mid.astype(F32)).astype(BF16)

    seq_len = lf.shape[0]
    row = lax.broadcasted_iota(jnp.int32, (MXU_DIM, MXU_DIM), 0)
    col = lax.broadcasted_iota(jnp.int32, (MXU_DIM, MXU_DIM), 1)
    tril = (col <= row).astype(BF16)
    head = lax.broadcasted_iota(jnp.int32, (N_HEADS, LANES), 0)
    lane = lax.broadcasted_iota(jnp.int32, (N_HEADS, LANES), 1)
    place = [(lane == 3 * head + part).astype(BF16) for part in range(3)]
    parts = split3(lf)
    carry = jnp.zeros((1, N_HEADS), F32)
    out = []
    for b in range(seq_len // MXU_DIM):
        rows = slice(b * MXU_DIM, (b + 1) * MXU_DIM)
        c = carry
        for part in parts:
            c = c + _dot(tril, part[rows])
        carry = c[MXU_DIM - 1:MXU_DIM]
        out.append(sum(_dot(t, sel) for t, sel in zip(split3(-c * LOG2_E), place)).astype(BF16))
    return out


def _attn_prompt_kernel(q_ref, k_ref, v_ref, lf_ref, sga_ref, o_ref,
                        vt_ref, aug_ref, qt_ref, m_ref, acc_ref, st_ref, p_ref, alpha_ref):
    grp = pl.program_id(1)
    qi = pl.program_id(2)
    tq = q_ref.shape[1]
    tk = MXU_DIM
    n_heads = vt_ref.shape[0]
    n_pairs = n_heads // HEADS_PER_BLOCK

    @pl.when((grp == 0) & (qi == 0))
    def _():
        for b, blk in enumerate(_bias_columns(lf_ref[0])):
            aug_ref[b * MXU_DIM:(b + 1) * MXU_DIM, :] = blk

    @pl.when(qi == 0)
    def _():
        ones = jnp.ones((vt_ref.shape[2] - HEAD_DIM, tk), BF16)
        for h in range(n_heads):
            for c in range(vt_ref.shape[1]):
                vt_ref[h, c, :HEAD_DIM, :] = v_ref[0, h * HEAD_DIM:(h + 1) * HEAD_DIM, c * tk:(c + 1) * tk]
                vt_ref[h, c, HEAD_DIM:, :] = ones

    row = lax.broadcasted_iota(jnp.int32, (LANES, tq), 0)
    for pair in range(n_pairs):
        qt = q_ref[0, :, _lane_chunk(pair)].astype(F32).T
        for half in range(HEADS_PER_BLOCK):
            h = pair * HEADS_PER_BLOCK + half
            own = (row >= half * HEAD_DIM) & (row < (half + 1) * HEAD_DIM)
            qt_ref[h, :LANES, :] = jnp.where(own, qt, 0.0).astype(BF16)
            lo = 3 * (grp * n_heads + h)
            qt_ref[h, LANES:, :] = ((row >= lo) & (row < lo + 3)).astype(BF16)
    m_ref[...] = jnp.full(m_ref.shape, MASK_VALUE, F32)
    acc_ref[...] = jnp.zeros(acc_ref.shape, F32)

    def scores(h, j, n_sub):
        rows = n_sub * tk
        k0 = pl.multiple_of(j * tk, tk)
        keys = jnp.concatenate([k_ref[0, pl.ds(k0, rows), _lane_chunk(h // HEADS_PER_BLOCK)],
                                aug_ref[pl.ds(k0, rows), :]], axis=1)
        st_ref[h, :rows, :] = _dot(keys, qt_ref[h])

    def softmax(h, n_sub, mask_last):
        rows = n_sub * tk
        diag0 = (n_sub - 1) * tk
        for c in range(tq // LANES):
            cols = _lane_chunk(c)

            def block(r):
                st = st_ref[h, r:r + SOFTMAX_ROWS, cols]
                if mask_last and r >= diag0:
                    key = lax.broadcasted_iota(jnp.int32, st.shape, 0) + (r - diag0)
                    qry = lax.broadcasted_iota(jnp.int32, st.shape, 1) + c * LANES
                    st = jnp.where(key <= qry, st, MASK_VALUE)
                return st

            m8 = None
            for r in range(0, rows, SOFTMAX_ROWS):
                bm = jnp.max(block(r).reshape(SOFTMAX_ROWS // 8, 8, LANES), axis=0)
                m8 = bm if m8 is None else jnp.maximum(m8, bm)
            m_prev = m_ref[h, :, cols]
            m_new = jnp.maximum(m_prev, jnp.max(m8, axis=0, keepdims=True))
            alpha_ref[h, :, cols] = jnp.exp2(m_prev - m_new)
            m_ref[h, :, cols] = m_new
            for r in range(0, rows, SOFTMAX_ROWS):
                p_ref[h, r:r + SOFTMAX_ROWS, cols] = jnp.exp2(block(r) - m_new).astype(BF16)

    def values(h, j, n_sub):
        pv = _dot(vt_ref[h, j], p_ref[h, :tk, :])
        for s in range(1, n_sub):
            pv = pv + _dot(vt_ref[h, j + s], p_ref[h, s * tk:(s + 1) * tk, :])
        acc_ref[h] = alpha_ref[h] * acc_ref[h] + pv

    def unit(j, n_sub, mask_last):
        for h in range(n_heads):
            scores(h, j, n_sub)
        for h in range(n_heads):
            softmax(h, n_sub, mask_last)
        for h in range(n_heads):
            values(h, j, n_sub)

    def body(jj, carry):
        unit(jj * KEY_CHUNKS_PER_UNIT, KEY_CHUNKS_PER_UNIT, False)
        return carry

    n_full = (qi + 1) // KEY_CHUNKS_PER_UNIT
    n_tail = (qi + 1) % KEY_CHUNKS_PER_UNIT
    lax.fori_loop(0, n_full - (n_tail == 0).astype(jnp.int32), body, 0)
    for tail in range(KEY_CHUNKS_PER_UNIT):
        n_sub = tail if tail else KEY_CHUNKS_PER_UNIT

        @pl.when(n_tail == tail)
        def _():
            unit(qi + 1 - n_sub, n_sub, True)

    for pair in range(n_pairs):
        lanes = _lane_chunk(pair)
        ot = []
        for half in range(HEADS_PER_BLOCK):
            acc = acc_ref[pair * HEADS_PER_BLOCK + half]
            ot.append(acc[:HEAD_DIM] / acc[HEAD_DIM:HEAD_DIM + 1])
        o = jnp.concatenate(ot, axis=0).T
        o_ref[0, :, lanes] = (o * sga_ref[0, :, lanes].astype(F32)).astype(BF16)


def _attn_prompt(q, k, v, lf, sga):
    n_seq, seq_len, _ = q.shape
    tq = Q_TILE
    assert tq == MXU_DIM and seq_len % tq == 0
    n_q = seq_len // tq
    n_pairs = ATTN_PAIRS_PER_STEP
    n_heads = n_pairs * HEADS_PER_BLOCK
    width = n_pairs * LANES
    acc_rows = HEAD_DIM + BF16_SUBLANES
    unit_rows = KEY_CHUNKS_PER_UNIT * MXU_DIM
    qspec = pl.BlockSpec((1, tq, width), lambda b, g, qi: (b, qi, g))
    kspec = pl.BlockSpec((1, seq_len, width), lambda b, g, qi: (b, 0, g))
    vspec = pl.BlockSpec((1, width, seq_len), lambda b, g, qi: (b, g, 0))
    return pl.pallas_call(
        _attn_prompt_kernel,
        grid=(n_seq, D_MODEL // width, n_q),
        in_specs=[qspec, kspec, vspec,
                  pl.BlockSpec((1, seq_len, N_HEADS), lambda b, g, qi: (b, 0, 0)), qspec],
        out_specs=qspec,
        out_shape=jax.ShapeDtypeStruct(q.shape, BF16),
        scratch_shapes=[pltpu.VMEM((n_heads, n_q, acc_rows, tq), BF16),
                        pltpu.VMEM((seq_len, LANES), BF16),
                        pltpu.VMEM((n_heads, 2 * LANES, tq), BF16),
                        pltpu.VMEM((n_heads, 1, tq), F32),
                        pltpu.VMEM((n_heads, acc_rows, tq), F32),
                        pltpu.VMEM((n_heads, unit_rows, tq), F32),
                        pltpu.VMEM((n_heads, unit_rows, tq), BF16),
                        pltpu.VMEM((n_heads, 1, tq), F32)],
        compiler_params=pltpu.CompilerParams(dimension_semantics=("arbitrary",) * 3,
                                             vmem_limit_bytes=VMEM_LIMIT_BYTES),
        name="attn_prompt",
    )(q, k, v, lf, sga)


def _attn_sample_kernel(q_ref, kp_ref, vp_ref, kn_ref, vn_ref, lft_ref, sga_ref, o_ref, negc_ref):
    hp = pl.program_id(1)
    tq = q_ref.shape[1]
    past = kp_ref.shape[2]

    @pl.when(hp == 0)
    def _():
        for b, blk in enumerate(_neg_cumsum_blocks(lft_ref[0])):
            negc_ref[:, b * MXU_DIM:(b + 1) * MXU_DIM] = blk

    qq = _stack_heads(q_ref[0])
    nb0 = negc_ref[pl.ds(HEADS_PER_BLOCK * hp, 1), :]
    nb1 = negc_ref[pl.ds(HEADS_PER_BLOCK * hp + 1, 1), :]

    def biased(s, lo, width):
        return s + jnp.concatenate([jnp.broadcast_to(nb0[:, lo:lo + width], (tq, width)),
                                    jnp.broadcast_to(nb1[:, lo:lo + width], (tq, width))], axis=0)

    s_past = biased(_dot(qq, kp_ref[0].astype(BF16)), 0, past)
    s_new = _causal_mask(biased(_dot_nt(qq, kn_ref[0]), past, tq), tq)
    m = jnp.maximum(jnp.max(s_past, axis=1, keepdims=True), jnp.max(s_new, axis=1, keepdims=True))
    p_past = jnp.exp2(s_past - m)
    p_new = jnp.exp2(s_new - m)
    l = jnp.sum(p_past, axis=1, keepdims=True) + jnp.sum(p_new, axis=1, keepdims=True)
    acc = _dot_nt(p_past.astype(BF16), vp_ref[0].astype(BF16)) + _dot(p_new.astype(BF16), vn_ref[0])
    o = _unstack_heads(acc / l, tq)
    o_ref[0] = (o * sga_ref[0].astype(F32)).astype(BF16)


def _attn_sample(q, k_past, v_past, k_new, v_new, lft_all, sga):
    n_seq, tq, _ = q.shape
    past = k_past.shape[2]
    total = lft_all.shape[2]
    qspec = pl.BlockSpec((1, tq, LANES), lambda b, hp: (b, 0, hp))
    pspec = pl.BlockSpec((1, LANES, past), lambda b, hp: (b, hp, 0))
    return pl.pallas_call(
        _attn_sample_kernel,
        grid=(n_seq, N_HEAD_BLOCKS),
        in_specs=[qspec, pspec, pspec, qspec, qspec,
                  pl.BlockSpec((1, N_HEADS, total), lambda b, hp: (b, 0, 0)), qspec],
        out_specs=qspec,
        out_shape=jax.ShapeDtypeStruct(q.shape, BF16),
        scratch_shapes=[pltpu.VMEM((N_HEADS, total), F32)],
        compiler_params=pltpu.CompilerParams(dimension_semantics=("arbitrary",) * 2,
                                             vmem_limit_bytes=VMEM_LIMIT_BYTES),
        name="attn_sample",
    )(q, k_past, v_past, k_new, v_new, lft_all, sga)


def _out_kernel(x_ref, oa_ref, cg_ref, sma_ref, smb_ref, wpa_ref, wpb_ref, wout_ref, y_ref):
    ya = _dot(oa_ref[...], wpa_ref[...])
    yb = _dot(cg_ref[...], wpb_ref[...])
    m = sma_ref[...].astype(F32) * ya + smb_ref[...].astype(F32) * yb
    y_ref[...] = x_ref[...] + _dot(m.astype(BF16), wout_ref[...])


def _out_proj(x2d, oa, cg, sma, smb, p):
    rows = x2d.shape[0]
    tm = min(OUT_ROW_TILE, rows)
    row_spec = pl.BlockSpec((tm, D_MODEL), lambda i: (i, 0))
    weight_spec = pl.BlockSpec((D_MODEL, D_MODEL), lambda i: (0, 0))
    return pl.pallas_call(
        _out_kernel,
        grid=(rows // tm,),
        in_specs=[row_spec] * 5 + [weight_spec] * 3,
        out_specs=row_spec,
        out_shape=jax.ShapeDtypeStruct((rows, D_MODEL), F32),
        compiler_params=pltpu.CompilerParams(dimension_semantics=("arbitrary",),
                                             vmem_limit_bytes=VMEM_LIMIT_BYTES),
        name="out_proj",
    )(x2d, oa, cg, sma, smb, p["wpa"], p["wpb"], p["wout"])


def _lane_chunked(a):
    return a.reshape(a.shape[0], N_LANE_CHUNKS, LANES).transpose(1, 0, 2)


def _layer_params(norm_g, w_in, b_f, q_g, k_g, w_dw, b_dw, ln_g, ln_b, w_pa, w_pb, w_out):
    off_f = 3 * D_MODEL
    w_f = jnp.pad(w_in[:, off_f:off_f + N_HEADS], ((0, 0), (0, LANES - N_HEADS)))
    w_all = jnp.concatenate([w_in[:, :off_f], w_in[:, off_f + N_HEADS:], w_f], axis=1).astype(BF16)
    head_of_col = jnp.arange(D_MODEL) // HEAD_DIM
    e_mat = (head_of_col[:, None] == jnp.arange(LANES)[None, :]).astype(BF16)
    et_mat = jnp.concatenate([e_mat.T, e_mat.T], axis=0)
    return dict(
        ng=norm_g.reshape(1, D_MODEL), w_all=w_all,
        bf_pad=jnp.pad(b_f, (0, LANES - N_HEADS)).reshape(1, LANES),
        qg=(jnp.tile(q_g, N_HEADS) * (HEAD_DIM ** -0.5 * LOG2_E)).reshape(1, D_MODEL),
        kg=jnp.tile(k_g, N_HEADS).reshape(1, D_MODEL),
        e_mat=e_mat, et_mat=et_mat,
        wdw=_lane_chunked(jnp.pad(w_dw, ((0, HALO - CONV_K), (0, 0)))),
        bdw=_lane_chunked(b_dw.reshape(1, D_MODEL)),
        lng=ln_g.reshape(1, D_MODEL), lnb=ln_b.reshape(1, D_MODEL),
        wpa=w_pa.astype(BF16), wpb=w_pb.astype(BF16), wout=w_out.astype(BF16))


def _mixer_path(x, p, conv_state, cache=None):
    n_seq, seq_len, _ = x.shape
    x2d = x.reshape(n_seq * seq_len, D_MODEL)
    time_minor = cache is None
    q, k, v, kb, vb, lf, lft, sga, cg, utail, sma, smb = _inproj(x2d, seq_len, conv_state, p, time_minor)
    seq3 = lambda a: a.reshape(n_seq, seq_len, D_MODEL)
    if cache is None:
        oa = _attn_prompt(seq3(q), seq3(kb), vb, lf.reshape(n_seq, seq_len, N_HEADS), seq3(sga))
        heads = lambda a: a.reshape(n_seq, N_HEADS, HEAD_DIM, seq_len).transpose(0, 3, 1, 2)
        lf_out = lft.transpose(0, 2, 1)
    else:
        k_past, v_past, lf_past = cache
        past = k_past.shape[1]
        channel_major = lambda a: a.transpose(0, 2, 3, 1).reshape(n_seq, D_MODEL, past)
        n_tiles, _, tm = lft.shape
        lft_new = lft.reshape(n_tiles, N_HEADS, tm // seq_len, seq_len).transpose(0, 2, 1, 3)
        lft_new = lft_new.reshape(n_seq, N_HEADS, seq_len)
        pad = -(past + seq_len) % MXU_DIM
        lft_all = jnp.pad(jnp.concatenate([lf_past.transpose(0, 2, 1), lft_new], axis=2),
                          ((0, 0), (0, 0), (0, pad)))
        oa = _attn_sample(seq3(q), channel_major(k_past), channel_major(v_past),
                          seq3(kb), seq3(vb), lft_all, seq3(sga))
        heads = lambda a: a.reshape(n_seq, seq_len, N_HEADS, HEAD_DIM)
        lf_out = lf.reshape(n_seq, seq_len, N_HEADS)
    y = _out_proj(x2d, oa.reshape(x2d.shape), cg, sma, smb, p)
    return y.reshape(x.shape), heads(k), heads(v), lf_out, utail[:, HALO - CONV_HIST:, :]


def kernel(x_prompt, x_sample, cache_k, cache_v, cache_logf, state_conv, norm_g, w_in, b_f, q_g, k_g,
           w_dw, b_dw, ln_g, ln_b, w_pa, w_pb, w_out):
    depth = w_in.shape[0]
    xp, xs = x_prompt, x_sample
    outs = [[] for _ in range(8)]
    for l in range(depth):
        p = _layer_params(norm_g[l], w_in[l], b_f[l], q_g[l], k_g[l], w_dw[l], b_dw[l], ln_g[l], ln_b[l],
                          w_pa[l], w_pb[l], w_out[l])
        fresh_state = jnp.zeros((xp.shape[0], HALO, D_MODEL), F32)
        xp, kp, vp, fp, cp = _mixer_path(xp, p, fresh_state)
        state = jnp.pad(state_conv[l].astype(F32), ((0, 0), (HALO - CONV_HIST, 0), (0, 0)))
        xs, ks, vs, fs, cs = _mixer_path(xs, p, state, cache=(cache_k[l], cache_v[l], cache_logf[l].astype(F32)))
        for lst, val in zip(outs, (kp, vp, fp, cp, ks, vs, fs, cs)):
            lst.append(val)
    return (xp, xs) + tuple(jnp.stack(o) for o in outs)
```

```python
import functools

import jax
import jax.numpy as jnp
from jax import lax
from jax.experimental import pallas as pl
from jax.experimental.pallas import tpu as pltpu

F32 = jnp.float32
BF16 = jnp.bfloat16

D_MODEL = 1024
N_HEADS = 16
HEAD_DIM = 64
CONV_K = 31
CONV_HIST = CONV_K - 1
EPS = 1e-6

LANES = 128
BF16_SUBLANES = 16
HALO = 32
N_LANE_CHUNKS = D_MODEL // LANES
HEADS_PER_BLOCK = LANES // HEAD_DIM
N_HEAD_BLOCKS = N_HEADS // HEADS_PER_BLOCK
MXU_DIM = 256
SEC_Q, SEC_K, SEC_V, SEC_GA, SEC_UA, SEC_UB, SEC_GB, SEC_MA, SEC_MB = range(9)
N_SECTIONS = 9
W_COLS = N_SECTIONS * D_MODEL + LANES
MASK_VALUE = -1e30
LOG2_E = 1.4426950408889634
VMEM_LIMIT_BYTES = 56 * 1024 * 1024

INPROJ_ROW_TILE = 256
OUT_ROW_TILE = 512
Q_TILE = 256
ATTN_PAIRS_PER_STEP = 4
SOFTMAX_ROWS = 64
KEY_CHUNKS_PER_UNIT = 2


def _sigmoid(x):
    return 1.0 / (1.0 + jnp.exp(-x))


def _dot(a, b):
    return jnp.dot(a, b, preferred_element_type=F32)


def _dot_nt(a, b):
    return lax.dot_general(a, b, (((1,), (1,)), ((), ())), preferred_element_type=F32)


def _lane_chunk(c):
    return slice(c * LANES, (c + 1) * LANES)


def _inproj_kernel(x_ref, ng_ref, w_ref, bf_ref, qg_ref, kg_ref, e_ref, et_ref,
                   state_ref, wdw_ref, bdw_ref, lng_ref, lnb_ref,
                   q_ref, k_ref, v_ref, kb_ref, vb_ref, lf_ref, lft_ref, sga_ref, cg_ref, utail_ref,
                   sma_ref, smb_ref, ext_ref, conv_ref, *, seg, tiles_per_seq, time_minor):
    tm = x_ref.shape[0]
    n_seg = tm // seg
    stride = HALO + seg

    x = x_ref[...]
    ms = jnp.mean(x * x, axis=-1, keepdims=True)
    h = (x * lax.rsqrt(ms + EPS) * ng_ref[...]).astype(BF16)

    def proj(sec):
        return _dot(h, w_ref[:, sec * D_MODEL:(sec + 1) * D_MODEL])

    u = proj(SEC_UA) * _sigmoid(proj(SEC_UB))
    for s in range(n_seg):
        utail_ref[s] = u[(s + 1) * seg - HALO:(s + 1) * seg]
    if tiles_per_seq > 1:
        @pl.when(pl.program_id(0) % tiles_per_seq == 0)
        def _():
            for c in range(N_LANE_CHUNKS):
                ext_ref[c, :HALO, :] = state_ref[0, :, _lane_chunk(c)]
    else:
        for s in range(n_seg):
            for c in range(N_LANE_CHUNKS):
                ext_ref[c, s * stride:s * stride + HALO, :] = state_ref[s, :, _lane_chunk(c)]
    for s in range(n_seg):
        for c in range(N_LANE_CHUNKS):
            ext_ref[c, s * stride + HALO:(s + 1) * stride, :] = u[s * seg:(s + 1) * seg, _lane_chunk(c)]

    row_chunk = min(LANES, seg)

    def conv(c):
        for s in range(n_seg):
            for r0 in range(0, seg, row_chunk):
                base = s * stride + r0 + HALO - CONV_HIST
                acc = jnp.broadcast_to(bdw_ref[c], (row_chunk, LANES))
                for j in range(CONV_K):
                    acc = acc + wdw_ref[c, j:j + 1, :] * ext_ref[c, base + j:base + j + row_chunk, :]
                conv_ref[c, s * seg + r0:s * seg + r0 + row_chunk, :] = acc
        if tiles_per_seq > 1:
            ext_ref[c, :HALO, :] = ext_ref[c, seg:seg + HALO, :]

    def head_rmsnorm(y, g):
        ss = _dot((y * y).astype(BF16), e_ref[...])
        inv = lax.rsqrt(ss * (1.0 / HEAD_DIM) + EPS)
        hi = inv.astype(BF16)
        lo = (inv - hi.astype(F32)).astype(BF16)
        scale = _dot(jnp.concatenate([hi, lo], axis=1), et_ref[...])
        return y * scale * g

    def store_state(value, state_out_ref):
        if time_minor:
            state_out_ref[0] = value.T
        else:
            for hd in range(N_HEADS):
                state_out_ref[:, hd, :] = value[:, hd * HEAD_DIM:(hd + 1) * HEAD_DIM]

    conv(0)
    q_ref[...] = head_rmsnorm(proj(SEC_Q), qg_ref[...]).astype(BF16)
    conv(1)
    k = head_rmsnorm(proj(SEC_K), kg_ref[...])
    store_state(k, k_ref)
    kb_ref[...] = k.astype(BF16)
    conv(2)
    v = proj(SEC_V)
    store_state(v, v_ref)
    if time_minor:
        vb_ref[0] = v.T.astype(BF16)
    else:
        vb_ref[...] = v.astype(BF16)
    conv(3)

    z = _dot(h, w_ref[:, N_SECTIONS * D_MODEL:]) + bf_ref[...]
    lf = jnp.minimum(z, 0.0) - jnp.log1p(jnp.exp(-jnp.abs(z)))
    lf_ref[...] = lf[:, :N_HEADS]
    lft_ref[0] = lf.T[:N_HEADS, :]

    ga = proj(SEC_GA)
    sga_ref[...] = (ga * _sigmoid(ga)).astype(BF16)
    conv(4)
    sma_ref[...] = _sigmoid(proj(SEC_MA)).astype(BF16)
    conv(5)
    smb_ref[...] = _sigmoid(proj(SEC_MB)).astype(BF16)
    conv(6)

    gb = proj(SEC_GB)
    sgb = gb * _sigmoid(gb)
    conv(7)
    total = jnp.zeros((tm, 1), F32)
    for c in range(N_LANE_CHUNKS):
        total = total + jnp.sum(conv_ref[c], axis=-1, keepdims=True)
    mu = total * (1.0 / D_MODEL)
    sq = jnp.zeros((tm, 1), F32)
    for c in range(N_LANE_CHUNKS):
        xc = conv_ref[c] - mu
        sq = sq + jnp.sum(xc * xc, axis=-1, keepdims=True)
    rstd = lax.rsqrt(sq * (1.0 / D_MODEL) + EPS)
    for c in range(N_LANE_CHUNKS):
        lanes = _lane_chunk(c)
        y = (conv_ref[c] - mu) * rstd * lng_ref[:, lanes] + lnb_ref[:, lanes]
        cg_ref[:, lanes] = (y * _sigmoid(y) * sgb[:, lanes]).astype(BF16)


def _inproj(x2d, seq_len, conv_state, p, time_minor):
    rows = x2d.shape[0]
    tm = min(INPROJ_ROW_TILE, rows)
    n_tiles = rows // tm
    n_seq = rows // seq_len
    seg = min(seq_len, tm)
    seq_per_tile = tm // seg
    tiles_per_seq = seq_len // seg

    assert seq_per_tile == 1 or tiles_per_seq == 1
    row_spec = pl.BlockSpec((tm, D_MODEL), lambda i: (i, 0))
    seq_spec = pl.BlockSpec((seq_per_tile, HALO, D_MODEL), lambda i: (i // tiles_per_seq, 0, 0))
    const = lambda shape: pl.BlockSpec(shape, lambda i: (0,) * len(shape))
    bf_rows = jax.ShapeDtypeStruct((rows, D_MODEL), BF16)
    if time_minor:
        assert seq_per_tile == 1
        time_spec = lambda ch: pl.BlockSpec((1, ch, tm), lambda i: (i // tiles_per_seq, 0, i % tiles_per_seq))
        state_spec, vb_spec, lft_spec = time_spec(D_MODEL), time_spec(D_MODEL), time_spec(N_HEADS)
        state_shape = jax.ShapeDtypeStruct((n_seq, D_MODEL, seq_len), F32)
        vb_shape = jax.ShapeDtypeStruct((n_seq, D_MODEL, seq_len), BF16)
        lft_shape = jax.ShapeDtypeStruct((n_seq, N_HEADS, seq_len), F32)
    else:
        state_spec = pl.BlockSpec((tm, N_HEADS, HEAD_DIM), lambda i: (i, 0, 0))
        vb_spec = row_spec
        lft_spec = pl.BlockSpec((1, N_HEADS, tm), lambda i: (i, 0, 0))
        state_shape = jax.ShapeDtypeStruct((rows, N_HEADS, HEAD_DIM), F32)
        vb_shape = bf_rows
        lft_shape = jax.ShapeDtypeStruct((n_tiles, N_HEADS, tm), F32)
    return pl.pallas_call(
        functools.partial(_inproj_kernel, seg=seg, tiles_per_seq=tiles_per_seq, time_minor=time_minor),
        grid=(n_tiles,),
        in_specs=[row_spec, const((1, D_MODEL)),
                  pl.BlockSpec((D_MODEL, W_COLS), lambda i: (0, 0), pipeline_mode=pl.Buffered(1)),
                  const((1, LANES)), const((1, D_MODEL)), const((1, D_MODEL)),
                  const((D_MODEL, LANES)), const((2 * LANES, D_MODEL)),
                  seq_spec, const((N_LANE_CHUNKS, HALO, LANES)), const((N_LANE_CHUNKS, 1, LANES)),
                  const((1, D_MODEL)), const((1, D_MODEL))],
        out_specs=[row_spec, state_spec, state_spec, row_spec, vb_spec,
                   pl.BlockSpec((tm, N_HEADS), lambda i: (i, 0)), lft_spec,
                   row_spec, row_spec, seq_spec, row_spec, row_spec],
        out_shape=[bf_rows, state_shape, state_shape, bf_rows, vb_shape,
                   jax.ShapeDtypeStruct((rows, N_HEADS), F32), lft_shape,
                   bf_rows, bf_rows,
                   jax.ShapeDtypeStruct((n_seq, HALO, D_MODEL), F32),
                   bf_rows, bf_rows],
        scratch_shapes=[pltpu.VMEM((N_LANE_CHUNKS, seq_per_tile * (HALO + seg), LANES), F32),
                        pltpu.VMEM((N_LANE_CHUNKS, tm, LANES), F32)],
        compiler_params=pltpu.CompilerParams(dimension_semantics=("arbitrary",),
                                             vmem_limit_bytes=VMEM_LIMIT_BYTES),
        name="inproj",
    )(x2d, p["ng"], p["w_all"], p["bf_pad"], p["qg"], p["kg"], p["e_mat"], p["et_mat"],
      conv_state, p["wdw"], p["bdw"], p["lng"], p["lnb"])


def _neg_cumsum_blocks(lf):
    hi = lf.astype(BF16)
    r1 = lf - hi.astype(F32)
    mid = r1.astype(BF16)
    lo = (r1 - mid.astype(F32)).astype(BF16)
    row = lax.broadcasted_iota(jnp.int32, (MXU_DIM, MXU_DIM), 0)
    col = lax.broadcasted_iota(jnp.int32, (MXU_DIM, MXU_DIM), 1)
    tri = (row <= col).astype(BF16)
    carry = jnp.zeros((lf.shape[0], 1), F32)
    out = []
    for b in range(lf.shape[1] // MXU_DIM):
        sl = slice(b * MXU_DIM, (b + 1) * MXU_DIM)
        c = _dot(hi[:, sl], tri) + _dot(mid[:, sl], tri) + _dot(lo[:, sl], tri) + carry
        out.append(-c * LOG2_E)
        carry = c[:, MXU_DIM - 1:MXU_DIM]
    return out


def _stack_heads(q):
    lane = lax.broadcasted_iota(jnp.int32, q.shape, 1)
    zero = jnp.zeros_like(q)
    return jnp.concatenate([jnp.where(lane < HEAD_DIM, q, zero),
                            jnp.where(lane >= HEAD_DIM, q, zero)], axis=0)


def _unstack_heads(o, tq):
    lane = lax.broadcasted_iota(jnp.int32, (tq, LANES), 1)
    return jnp.where(lane < HEAD_DIM, o[:tq], o[tq:])


def _causal_mask(s, tq):
    row = lax.broadcasted_iota(jnp.int32, s.shape, 0)
    row = jnp.where(row >= tq, row - tq, row)
    col = lax.broadcasted_iota(jnp.int32, s.shape, 1)
    return jnp.where(col <= row, s, MASK_VALUE)


def _bias_columns(lf):
    def split3(a):
        hi = a.astype(BF16)
        r1 = a - hi.astype(F32)
        mid = r1.astype(BF16)
        return hi, mid, (r1 - mid.astype(F32)).astype(BF16)

    seq_len = lf.shape[0]
    row = lax.broadcasted_iota(jnp.int32, (MXU_DIM, MXU_DIM), 0)
    col = lax.broadcasted_iota(jnp.int32, (MXU_DIM, MXU_DIM), 1)
    tril = (col <= row).astype(BF16)
    head = lax.broadcasted_iota(jnp.int32, (N_HEADS, LANES), 0)
    lane = lax.broadcasted_iota(jnp.int32, (N_HEADS, LANES), 1)
    place = [(lane == 3 * head + part).astype(BF16) for part in range(3)]
    parts = split3(lf)
    carry = jnp.zeros((1, N_HEADS), F32)
    out = []
    for b in range(seq_len // MXU_DIM):
        rows = slice(b * MXU_DIM, (b + 1) * MXU_DIM)
        c = carry
        for part in parts:
            c = c + _dot(tril, part[rows])
        carry = c[MXU_DIM - 1:MXU_DIM]
        out.append(sum(_dot(t, sel) for t, sel in zip(split3(-c * LOG2_E), place)).astype(BF16))
    return out


def _attn_prompt_kernel(q_ref, k_ref, v_ref, lf_ref, sga_ref, o_ref,
                        vt_ref, aug_ref, qt_ref, m_ref, acc_ref, st_ref, p_ref, alpha_ref):
    grp = pl.program_id(1)
    qi = pl.program_id(2)
    tq = q_ref.shape[1]
    tk = MXU_DIM
    n_heads = vt_ref.shape[0]
    n_pairs = n_heads // HEADS_PER_BLOCK

    @pl.when((grp == 0) & (qi == 0))
    def _():
        for b, blk in enumerate(_bias_columns(lf_ref[0])):
            aug_ref[b * MXU_DIM:(b + 1) * MXU_DIM, :] = blk

    @pl.when(qi == 0)
    def _():
        ones = jnp.ones((vt_ref.shape[2] - HEAD_DIM, tk), BF16)
        for h in range(n_heads):
            for c in range(vt_ref.shape[1]):
                vt_ref[h, c, :HEAD_DIM, :] = v_ref[0, h * HEAD_DIM:(h + 1) * HEAD_DIM, c * tk:(c + 1) * tk]
                vt_ref[h, c, HEAD_DIM:, :] = ones

    row = lax.broadcasted_iota(jnp.int32, (LANES, tq), 0)
    for pair in range(n_pairs):
        qt = q_ref[0, :, _lane_chunk(pair)].astype(F32).T
        for half in range(HEADS_PER_BLOCK):
            h = pair * HEADS_PER_BLOCK + half
            own = (row >= half * HEAD_DIM) & (row < (half + 1) * HEAD_DIM)
            qt_ref[h, :LANES, :] = jnp.where(own, qt, 0.0).astype(BF16)
            lo = 3 * (grp * n_heads + h)
            qt_ref[h, LANES:, :] = ((row >= lo) & (row < lo + 3)).astype(BF16)
    m_ref[...] = jnp.full(m_ref.shape, MASK_VALUE, F32)
    acc_ref[...] = jnp.zeros(acc_ref.shape, F32)

    def scores(h, j, n_sub):
        rows = n_sub * tk
        k0 = pl.multiple_of(j * tk, tk)
        keys = jnp.concatenate([k_ref[0, pl.ds(k0, rows), _lane_chunk(h // HEADS_PER_BLOCK)],
                                aug_ref[pl.ds(k0, rows), :]], axis=1)
        st_ref[h, :rows, :] = _dot(keys, qt_ref[h])

    def softmax(h, n_sub, mask_last):
        rows = n_sub * tk
        diag0 = (n_sub - 1) * tk
        for c in range(tq // LANES):
            cols = _lane_chunk(c)

            def block(r):
                st = st_ref[h, r:r + SOFTMAX_ROWS, cols]
                if mask_last and r >= diag0:
                    key = lax.broadcasted_iota(jnp.int32, st.shape, 0) + (r - diag0)
                    qry = lax.broadcasted_iota(jnp.int32, st.shape, 1) + c * LANES
                    st = jnp.where(key <= qry, st, MASK_VALUE)
                return st

            m8 = None
            for r in range(0, rows, SOFTMAX_ROWS):
                bm = jnp.max(block(r).reshape(SOFTMAX_ROWS // 8, 8, LANES), axis=0)
                m8 = bm if m8 is None else jnp.maximum(m8, bm)
            m_prev = m_ref[h, :, cols]
            m_new = jnp.maximum(m_prev, jnp.max(m8, axis=0, keepdims=True))
            alpha_ref[h, :, cols] = jnp.exp2(m_prev - m_new)
            m_ref[h, :, cols] = m_new
            for r in range(0, rows, SOFTMAX_ROWS):
                p_ref[h, r:r + SOFTMAX_ROWS, cols] = jnp.exp2(block(r) - m_new).astype(BF16)

    def values(h, j, n_sub):
        pv = _dot(vt_ref[h, j], p_ref[h, :tk, :])
        for s in range(1, n_sub):
            pv = pv + _dot(vt_ref[h, j + s], p_ref[h, s * tk:(s + 1) * tk, :])
        acc_ref[h] = alpha_ref[h] * acc_ref[h] + pv

    def unit(j, n_sub, mask_last):
        for h in range(n_heads):
            scores(h, j, n_sub)
        for h in range(n_heads):
            softmax(h, n_sub, mask_last)
        for h in range(n_heads):
            values(h, j, n_sub)

    for v in range(vt_ref.shape[1]):
        @pl.when(qi == v)
        def _():
            j = 0
            while v + 1 - j > KEY_CHUNKS_PER_UNIT:
                unit(j, KEY_CHUNKS_PER_UNIT, False)
                j += KEY_CHUNKS_PER_UNIT
            unit(j, v + 1 - j, True)

    for pair in range(n_pairs):
        lanes = _lane_chunk(pair)
        ot = []
        for half in range(HEADS_PER_BLOCK):
            acc = acc_ref[pair * HEADS_PER_BLOCK + half]
            ot.append(acc[:HEAD_DIM] / acc[HEAD_DIM:HEAD_DIM + 1])
        o = jnp.concatenate(ot, axis=0).T
        o_ref[0, :, lanes] = (o * sga_ref[0, :, lanes].astype(F32)).astype(BF16)


def _attn_prompt(q, k, v, lf, sga):
    n_seq, seq_len, _ = q.shape
    tq = Q_TILE
    assert tq == MXU_DIM and seq_len % tq == 0
    n_q = seq_len // tq
    n_pairs = ATTN_PAIRS_PER_STEP
    n_heads = n_pairs * HEADS_PER_BLOCK
    width = n_pairs * LANES
    acc_rows = HEAD_DIM + BF16_SUBLANES
    unit_rows = KEY_CHUNKS_PER_UNIT * MXU_DIM
    qspec = pl.BlockSpec((1, tq, width), lambda b, g, qi: (b, qi, g))
    kspec = pl.BlockSpec((1, seq_len, width), lambda b, g, qi: (b, 0, g))
    vspec = pl.BlockSpec((1, width, seq_len), lambda b, g, qi: (b, g, 0))
    return pl.pallas_call(
        _attn_prompt_kernel,
        grid=(n_seq, D_MODEL // width, n_q),
        in_specs=[qspec, kspec, vspec,
                  pl.BlockSpec((1, seq_len, N_HEADS), lambda b, g, qi: (b, 0, 0)), qspec],
        out_specs=qspec,
        out_shape=jax.ShapeDtypeStruct(q.shape, BF16),
        scratch_shapes=[pltpu.VMEM((n_heads, n_q, acc_rows, tq), BF16),
                        pltpu.VMEM((seq_len, LANES), BF16),
                        pltpu.VMEM((n_heads, 2 * LANES, tq), BF16),
                        pltpu.VMEM((n_heads, 1, tq), F32),
                        pltpu.VMEM((n_heads, acc_rows, tq), F32),
                        pltpu.VMEM((n_heads, unit_rows, tq), F32),
                        pltpu.VMEM((n_heads, unit_rows, tq), BF16),
                        pltpu.VMEM((n_heads, 1, tq), F32)],
        compiler_params=pltpu.CompilerParams(dimension_semantics=("arbitrary",) * 3,
                                             vmem_limit_bytes=VMEM_LIMIT_BYTES),
        name="attn_prompt",
    )(q, k, v, lf, sga)


def _attn_sample_kernel(q_ref, kp_ref, vp_ref, kn_ref, vn_ref, lft_ref, sga_ref, o_ref, negc_ref):
    hp = pl.program_id(1)
    tq = q_ref.shape[1]
    past = kp_ref.shape[2]

    @pl.when(hp == 0)
    def _():
        for b, blk in enumerate(_neg_cumsum_blocks(lft_ref[0])):
            negc_ref[:, b * MXU_DIM:(b + 1) * MXU_DIM] = blk

    qq = _stack_heads(q_ref[0])
    nb0 = negc_ref[pl.ds(HEADS_PER_BLOCK * hp, 1), :]
    nb1 = negc_ref[pl.ds(HEADS_PER_BLOCK * hp + 1, 1), :]

    def biased(s, lo, width):
        return s + jnp.concatenate([jnp.broadcast_to(nb0[:, lo:lo + width], (tq, width)),
                                    jnp.broadcast_to(nb1[:, lo:lo + width], (tq, width))], axis=0)

    s_past = biased(_dot(qq, kp_ref[0].astype(BF16)), 0, past)
    s_new = _causal_mask(biased(_dot_nt(qq, kn_ref[0]), past, tq), tq)
    m = jnp.maximum(jnp.max(s_past, axis=1, keepdims=True), jnp.max(s_new, axis=1, keepdims=True))
    p_past = jnp.exp2(s_past - m)
    p_new = jnp.exp2(s_new - m)
    l = jnp.sum(p_past, axis=1, keepdims=True) + jnp.sum(p_new, axis=1, keepdims=True)
    acc = _dot_nt(p_past.astype(BF16), vp_ref[0].astype(BF16)) + _dot(p_new.astype(BF16), vn_ref[0])
    o = _unstack_heads(acc / l, tq)
    o_ref[0] = (o * sga_ref[0].astype(F32)).astype(BF16)


def _attn_sample(q, k_past, v_past, k_new, v_new, lft_all, sga):
    n_seq, tq, _ = q.shape
    past = k_past.shape[2]
    total = lft_all.shape[2]
    qspec = pl.BlockSpec((1, tq, LANES), lambda b, hp: (b, 0, hp))
    pspec = pl.BlockSpec((1, LANES, past), lambda b, hp: (b, hp, 0))
    return pl.pallas_call(
        _attn_sample_kernel,
        grid=(n_seq, N_HEAD_BLOCKS),
        in_specs=[qspec, pspec, pspec, qspec, qspec,
                  pl.BlockSpec((1, N_HEADS, total), lambda b, hp: (b, 0, 0)), qspec],
        out_specs=qspec,
        out_shape=jax.ShapeDtypeStruct(q.shape, BF16),
        scratch_shapes=[pltpu.VMEM((N_HEADS, total), F32)],
        compiler_params=pltpu.CompilerParams(dimension_semantics=("arbitrary",) * 2,
                                             vmem_limit_bytes=VMEM_LIMIT_BYTES),
        name="attn_sample",
    )(q, k_past, v_past, k_new, v_new, lft_all, sga)


def _out_kernel(x_ref, oa_ref, cg_ref, sma_ref, smb_ref, wpa_ref, wpb_ref, wout_ref, y_ref):
    ya = _dot(oa_ref[...], wpa_ref[...])
    yb = _dot(cg_ref[...], wpb_ref[...])
    m = sma_ref[...].astype(F32) * ya + smb_ref[...].astype(F32) * yb
    y_ref[...] = x_ref[...] + _dot(m.astype(BF16), wout_ref[...])


def _out_proj(x2d, oa, cg, sma, smb, p):
    rows = x2d.shape[0]
    tm = min(OUT_ROW_TILE, rows)
    row_spec = pl.BlockSpec((tm, D_MODEL), lambda i: (i, 0))
    weight_spec = pl.BlockSpec((D_MODEL, D_MODEL), lambda i: (0, 0))
    return pl.pallas_call(
        _out_kernel,
        grid=(rows // tm,),
        in_specs=[row_spec] * 5 + [weight_spec] * 3,
        out_specs=row_spec,
        out_shape=jax.ShapeDtypeStruct((rows, D_MODEL), F32),
        compiler_params=pltpu.CompilerParams(dimension_semantics=("arbitrary",),
                                             vmem_limit_bytes=VMEM_LIMIT_BYTES),
        name="out_proj",
    )(x2d, oa, cg, sma, smb, p["wpa"], p["wpb"], p["wout"])


def _lane_chunked(a):
    return a.reshape(a.shape[0], N_LANE_CHUNKS, LANES).transpose(1, 0, 2)


def _layer_params(norm_g, w_in, b_f, q_g, k_g, w_dw, b_dw, ln_g, ln_b, w_pa, w_pb, w_out):
    off_f = 3 * D_MODEL
    w_f = jnp.pad(w_in[:, off_f:off_f + N_HEADS], ((0, 0), (0, LANES - N_HEADS)))
    w_all = jnp.concatenate([w_in[:, :off_f], w_in[:, off_f + N_HEADS:], w_f], axis=1).astype(BF16)
    head_of_col = jnp.arange(D_MODEL) // HEAD_DIM
    e_mat = (head_of_col[:, None] == jnp.arange(LANES)[None, :]).astype(BF16)
    et_mat = jnp.concatenate([e_mat.T, e_mat.T], axis=0)
    return dict(
        ng=norm_g.reshape(1, D_MODEL), w_all=w_all,
        bf_pad=jnp.pad(b_f, (0, LANES - N_HEADS)).reshape(1, LANES),
        qg=(jnp.tile(q_g, N_HEADS) * (HEAD_DIM ** -0.5 * LOG2_E)).reshape(1, D_MODEL),
        kg=jnp.tile(k_g, N_HEADS).reshape(1, D_MODEL),
        e_mat=e_mat, et_mat=et_mat,
        wdw=_lane_chunked(jnp.pad(w_dw, ((0, HALO - CONV_K), (0, 0)))),
        bdw=_lane_chunked(b_dw.reshape(1, D_MODEL)),
        lng=ln_g.reshape(1, D_MODEL), lnb=ln_b.reshape(1, D_MODEL),
        wpa=w_pa.astype(BF16), wpb=w_pb.astype(BF16), wout=w_out.astype(BF16))


def _mixer_path(x, p, conv_state, cache=None):
    n_seq, seq_len, _ = x.shape
    x2d = x.reshape(n_seq * seq_len, D_MODEL)
    time_minor = cache is None
    q, k, v, kb, vb, lf, lft, sga, cg, utail, sma, smb = _inproj(x2d, seq_len, conv_state, p, time_minor)
    seq3 = lambda a: a.reshape(n_seq, seq_len, D_MODEL)
    if cache is None:
        oa = _attn_prompt(seq3(q), seq3(kb), vb, lf.reshape(n_seq, seq_len, N_HEADS), seq3(sga))
        heads = lambda a: a.reshape(n_seq, N_HEADS, HEAD_DIM, seq_len).transpose(0, 3, 1, 2)
        lf_out = lft.transpose(0, 2, 1)
    else:
        k_past, v_past, lf_past = cache
        past = k_past.shape[1]
        channel_major = lambda a: a.transpose(0, 2, 3, 1).reshape(n_seq, D_MODEL, past)
        n_tiles, _, tm = lft.shape
        lft_new = lft.reshape(n_tiles, N_HEADS, tm // seq_len, seq_len).transpose(0, 2, 1, 3)
        lft_new = lft_new.reshape(n_seq, N_HEADS, seq_len)
        pad = -(past + seq_len) % MXU_DIM
        lft_all = jnp.pad(jnp.concatenate([lf_past.transpose(0, 2, 1), lft_new], axis=2),
                          ((0, 0), (0, 0), (0, pad)))
        oa = _attn_sample(seq3(q), channel_major(k_past), channel_major(v_past),
                          seq3(kb), seq3(vb), lft_all, seq3(sga))
        heads = lambda a: a.reshape(n_seq, seq_len, N_HEADS, HEAD_DIM)
        lf_out = lf.reshape(n_seq, seq_len, N_HEADS)
    y = _out_proj(x2d, oa.reshape(x2d.shape), cg, sma, smb, p)
    return y.reshape(x.shape), heads(k), heads(v), lf_out, utail[:, HALO - CONV_HIST:, :]


def kernel(x_prompt, x_sample, cache_k, cache_v, cache_logf, state_conv, norm_g, w_in, b_f, q_g, k_g,
           w_dw, b_dw, ln_g, ln_b, w_pa, w_pb, w_out):
    depth = w_in.shape[0]
    xp, xs = x_prompt, x_sample
    outs = [[] for _ in range(8)]
    for l in range(depth):
        p = _layer_params(norm_g[l], w_in[l], b_f[l], q_g[l], k_g[l], w_dw[l], b_dw[l], ln_g[l], ln_b[l],
                          w_pa[l], w_pb[l], w_out[l])
        fresh_state = jnp.zeros((xp.shape[0], HALO, D_MODEL), F32)
        xp, kp, vp, fp, cp = _mixer_path(xp, p, fresh_state)
        state = jnp.pad(state_conv[l].astype(F32), ((0, 0), (HALO - CONV_HIST, 0), (0, 0)))
        xs, ks, vs, fs, cs = _mixer_path(xs, p, state, cache=(cache_k[l], cache_v[l], cache_logf[l].astype(F32)))
        for lst, val in zip(outs, (kp, vp, fp, cp, ks, vs, fs, cs)):
            lst.append(val)
    return (xp, xs) + tuple(jnp.stack(o) for o in outs)
```

```python
import functools

import jax
import jax.numpy as jnp
from jax import lax
from jax.experimental import pallas as pl
from jax.experimental.pallas import tpu as pltpu

F32 = jnp.float32
BF16 = jnp.bfloat16

D_MODEL = 1024
N_HEADS = 16
HEAD_DIM = 64
CONV_K = 31
CONV_HIST = CONV_K - 1
EPS = 1e-6

LANES = 128
BF16_SUBLANES = 16
HALO = 32
N_LANE_CHUNKS = D_MODEL // LANES
HEADS_PER_BLOCK = LANES // HEAD_DIM
N_HEAD_BLOCKS = N_HEADS // HEADS_PER_BLOCK
MXU_DIM = 256
SEC_Q, SEC_K, SEC_V, SEC_GA, SEC_UA, SEC_UB, SEC_GB, SEC_MA, SEC_MB = range(9)
N_SECTIONS = 9
W_COLS = N_SECTIONS * D_MODEL + LANES
MASK_VALUE = -1e30
LOG2_E = 1.4426950408889634
VMEM_LIMIT_BYTES = 56 * 1024 * 1024

INPROJ_ROW_TILE = 256
OUT_ROW_TILE = 512
Q_TILE = 256
ATTN_PAIRS_PER_STEP = 4
SOFTMAX_ROWS = 64
KEY_CHUNKS_PER_UNIT = 2


def _sigmoid(x):
    return 1.0 / (1.0 + jnp.exp(-x))


def _dot(a, b):
    return jnp.dot(a, b, preferred_element_type=F32)


def _dot_nt(a, b):
    return lax.dot_general(a, b, (((1,), (1,)), ((), ())), preferred_element_type=F32)


def _lane_chunk(c):
    return slice(c * LANES, (c + 1) * LANES)


def _inproj_kernel(x_ref, ng_ref, w_ref, bf_ref, qg_ref, kg_ref, e_ref, et_ref,
                   state_ref, wdw_ref, bdw_ref, lng_ref, lnb_ref,
                   q_ref, k_ref, v_ref, kb_ref, vb_ref, lf_ref, lft_ref, sga_ref, cg_ref, utail_ref,
                   sma_ref, smb_ref, ext_ref, conv_ref, *, seg, tiles_per_seq, time_minor):
    tm = x_ref.shape[0]
    n_seg = tm // seg
    stride = HALO + seg

    x = x_ref[...]
    ms = jnp.mean(x * x, axis=-1, keepdims=True)
    h = (x * lax.rsqrt(ms + EPS) * ng_ref[...]).astype(BF16)

    def proj(sec):
        return _dot(h, w_ref[:, sec * D_MODEL:(sec + 1) * D_MODEL])

    u = proj(SEC_UA) * _sigmoid(proj(SEC_UB))
    for s in range(n_seg):
        utail_ref[s] = u[(s + 1) * seg - HALO:(s + 1) * seg]
    if tiles_per_seq > 1:
        @pl.when(pl.program_id(0) % tiles_per_seq == 0)
        def _():
            for c in range(N_LANE_CHUNKS):
                ext_ref[c, :HALO, :] = state_ref[0, :, _lane_chunk(c)]
    else:
        for s in range(n_seg):
            for c in range(N_LANE_CHUNKS):
                ext_ref[c, s * stride:s * stride + HALO, :] = state_ref[s, :, _lane_chunk(c)]
    for s in range(n_seg):
        for c in range(N_LANE_CHUNKS):
            ext_ref[c, s * stride + HALO:(s + 1) * stride, :] = u[s * seg:(s + 1) * seg, _lane_chunk(c)]

    row_chunk = min(LANES, seg)

    def conv(c):
        for s in range(n_seg):
            for r0 in range(0, seg, row_chunk):
                base = s * stride + r0 + HALO - CONV_HIST
                acc = jnp.broadcast_to(bdw_ref[c], (row_chunk, LANES))
                for j in range(CONV_K):
                    acc = acc + wdw_ref[c, j:j + 1, :] * ext_ref[c, base + j:base + j + row_chunk, :]
                conv_ref[c, s * seg + r0:s * seg + r0 + row_chunk, :] = acc
        if tiles_per_seq > 1:
            ext_ref[c, :HALO, :] = ext_ref[c, seg:seg + HALO, :]

    def head_rmsnorm(y, g):
        ss = _dot((y * y).astype(BF16), e_ref[...])
        inv = lax.rsqrt(ss * (1.0 / HEAD_DIM) + EPS)
        hi = inv.astype(BF16)
        lo = (inv - hi.astype(F32)).astype(BF16)
        scale = _dot(jnp.concatenate([hi, lo], axis=1), et_ref[...])
        return y * scale * g

    def store_state(value, state_out_ref):
        if time_minor:
            state_out_ref[0] = value.T
        else:
            for hd in range(N_HEADS):
                state_out_ref[:, hd, :] = value[:, hd * HEAD_DIM:(hd + 1) * HEAD_DIM]

    conv(0)
    q_ref[...] = head_rmsnorm(proj(SEC_Q), qg_ref[...]).astype(BF16)
    conv(1)
    k = head_rmsnorm(proj(SEC_K), kg_ref[...])
    store_state(k, k_ref)
    kb_ref[...] = k.astype(BF16)
    conv(2)
    v = proj(SEC_V)
    store_state(v, v_ref)
    if time_minor:
        vb_ref[0] = v.T.astype(BF16)
    else:
        vb_ref[...] = v.astype(BF16)
    conv(3)

    z = _dot(h, w_ref[:, N_SECTIONS * D_MODEL:]) + bf_ref[...]
    lf = jnp.minimum(z, 0.0) - jnp.log1p(jnp.exp(-jnp.abs(z)))
    lf_ref[...] = lf[:, :N_HEADS]
    lft_ref[0] = lf.T[:N_HEADS, :]

    ga = proj(SEC_GA)
    sga_ref[...] = (ga * _sigmoid(ga)).astype(BF16)
    conv(4)
    sma_ref[...] = _sigmoid(proj(SEC_MA)).astype(BF16)
    conv(5)
    smb_ref[...] = _sigmoid(proj(SEC_MB)).astype(BF16)
    conv(6)

    gb = proj(SEC_GB)
    sgb = gb * _sigmoid(gb)
    conv(7)
    total = jnp.zeros((tm, 1), F32)
    for c in range(N_LANE_CHUNKS):
        total = total + jnp.sum(conv_ref[c], axis=-1, keepdims=True)
    mu = total * (1.0 / D_MODEL)
    sq = jnp.zeros((tm, 1), F32)
    for c in range(N_LANE_CHUNKS):
        xc = conv_ref[c] - mu
        sq = sq + jnp.sum(xc * xc, axis=-1, keepdims=True)
    rstd = lax.rsqrt(sq * (1.0 / D_MODEL) + EPS)
    for c in range(N_LANE_CHUNKS):
        lanes = _lane_chunk(c)
        y = (conv_ref[c] - mu) * rstd * lng_ref[:, lanes] + lnb_ref[:, lanes]
        cg_ref[:, lanes] = (y * _sigmoid(y) * sgb[:, lanes]).astype(BF16)


def _inproj(x2d, seq_len, conv_state, p, time_minor):
    rows = x2d.shape[0]
    tm = min(INPROJ_ROW_TILE, rows)
    n_tiles = rows // tm
    n_seq = rows // seq_len
    seg = min(seq_len, tm)
    seq_per_tile = tm // seg
    tiles_per_seq = seq_len // seg

    assert seq_per_tile == 1 or tiles_per_seq == 1
    row_spec = pl.BlockSpec((tm, D_MODEL), lambda i: (i, 0))
    seq_spec = pl.BlockSpec((seq_per_tile, HALO, D_MODEL), lambda i: (i // tiles_per_seq, 0, 0))
    const = lambda shape: pl.BlockSpec(shape, lambda i: (0,) * len(shape))
    bf_rows = jax.ShapeDtypeStruct((rows, D_MODEL), BF16)
    if time_minor:
        assert seq_per_tile == 1
        time_spec = lambda ch: pl.BlockSpec((1, ch, tm), lambda i: (i // tiles_per_seq, 0, i % tiles_per_seq))
        state_spec, vb_spec, lft_spec = time_spec(D_MODEL), time_spec(D_MODEL), time_spec(N_HEADS)
        state_shape = jax.ShapeDtypeStruct((n_seq, D_MODEL, seq_len), F32)
        vb_shape = jax.ShapeDtypeStruct((n_seq, D_MODEL, seq_len), BF16)
        lft_shape = jax.ShapeDtypeStruct((n_seq, N_HEADS, seq_len), F32)
    else:
        state_spec = pl.BlockSpec((tm, N_HEADS, HEAD_DIM), lambda i: (i, 0, 0))
        vb_spec = row_spec
        lft_spec = pl.BlockSpec((1, N_HEADS, tm), lambda i: (i, 0, 0))
        state_shape = jax.ShapeDtypeStruct((rows, N_HEADS, HEAD_DIM), F32)
        vb_shape = bf_rows
        lft_shape = jax.ShapeDtypeStruct((n_tiles, N_HEADS, tm), F32)
    return pl.pallas_call(
        functools.partial(_inproj_kernel, seg=seg, tiles_per_seq=tiles_per_seq, time_minor=time_minor),
        grid=(n_tiles,),
        in_specs=[row_spec, const((1, D_MODEL)),
                  pl.BlockSpec((D_MODEL, W_COLS), lambda i: (0, 0), pipeline_mode=pl.Buffered(1)),
                  const((1, LANES)), const((1, D_MODEL)), const((1, D_MODEL)),
                  const((D_MODEL, LANES)), const((2 * LANES, D_MODEL)),
                  seq_spec, const((N_LANE_CHUNKS, HALO, LANES)), const((N_LANE_CHUNKS, 1, LANES)),
                  const((1, D_MODEL)), const((1, D_MODEL))],
        out_specs=[row_spec, state_spec, state_spec, row_spec, vb_spec,
                   pl.BlockSpec((tm, N_HEADS), lambda i: (i, 0)), lft_spec,
                   row_spec, row_spec, seq_spec, row_spec, row_spec],
        out_shape=[bf_rows, state_shape, state_shape, bf_rows, vb_shape,
                   jax.ShapeDtypeStruct((rows, N_HEADS), F32), lft_shape,
                   bf_rows, bf_rows,
                   jax.ShapeDtypeStruct((n_seq, HALO, D_MODEL), F32),
                   bf_rows, bf_rows],
        scratch_shapes=[pltpu.VMEM((N_LANE_CHUNKS, seq_per_tile * (HALO + seg), LANES), F32),
                        pltpu.VMEM((N_LANE_CHUNKS, tm, LANES), F32)],
        compiler_params=pltpu.CompilerParams(dimension_semantics=("arbitrary",),
                                             vmem_limit_bytes=VMEM_LIMIT_BYTES),
        name="inproj",
    )(x2d, p["ng"], p["w_all"], p["bf_pad"], p["qg"], p["kg"], p["e_mat"], p["et_mat"],
      conv_state, p["wdw"], p["bdw"], p["lng"], p["lnb"])


def _neg_cumsum_blocks(lf):
    hi = lf.astype(BF16)
    r1 = lf - hi.astype(F32)
    mid = r1.astype(BF16)
    lo = (r1 - mid.astype(F32)).astype(BF16)
    row = lax.broadcasted_iota(jnp.int32, (MXU_DIM, MXU_DIM), 0)
    col = lax.broadcasted_iota(jnp.int32, (MXU_DIM, MXU_DIM), 1)
    tri = (row <= col).astype(BF16)
    carry = jnp.zeros((lf.shape[0], 1), F32)
    out = []
    for b in range(lf.shape[1] // MXU_DIM):
        sl = slice(b * MXU_DIM, (b + 1) * MXU_DIM)
        c = _dot(hi[:, sl], tri) + _dot(mid[:, sl], tri) + _dot(lo[:, sl], tri) + carry
        out.append(-c * LOG2_E)
        carry = c[:, MXU_DIM - 1:MXU_DIM]
    return out


def _stack_heads(q):
    lane = lax.broadcasted_iota(jnp.int32, q.shape, 1)
    zero = jnp.zeros_like(q)
    return jnp.concatenate([jnp.where(lane < HEAD_DIM, q, zero),
                            jnp.where(lane >= HEAD_DIM, q, zero)], axis=0)


def _unstack_heads(o, tq):
    lane = lax.broadcasted_iota(jnp.int32, (tq, LANES), 1)
    return jnp.where(lane < HEAD_DIM, o[:tq], o[tq:])


def _causal_mask(s, tq):
    row = lax.broadcasted_iota(jnp.int32, s.shape, 0)
    row = jnp.where(row >= tq, row - tq, row)
    col = lax.broadcasted_iota(jnp.int32, s.shape, 1)
    return jnp.where(col <= row, s, MASK_VALUE)


def _bias_columns(lf):
    def split3(a):
        hi = a.astype(BF16)
        r1 = a - hi.astype(F32)
        mid = r1.astype(BF16)
        return hi, mid, (r1 - mid.astype(F32)).astype(BF16)

    seq_len = lf.shape[0]
    row = lax.broadcasted_iota(jnp.int32, (MXU_DIM, MXU_DIM), 0)
    col = lax.broadcasted_iota(jnp.int32, (MXU_DIM, MXU_DIM), 1)
    tril = (col <= row).astype(BF16)
    head = lax.broadcasted_iota(jnp.int32, (N_HEADS, LANES), 0)
    lane = lax.broadcasted_iota(jnp.int32, (N_HEADS, LANES), 1)
    place = [(lane == 3 * head + part).astype(BF16) for part in range(3)]
    parts = split3(lf)
    carry = jnp.zeros((1, N_HEADS), F32)
    out = []
    for b in range(seq_len // MXU_DIM):
        rows = slice(b * MXU_DIM, (b + 1) * MXU_DIM)
        c = carry
        for part in parts:
            c = c + _dot(tril, part[rows])
        carry = c[MXU_DIM - 1:MXU_DIM]
        out.append(sum(_dot(t, sel) for t, sel in zip(split3(-c * LOG2_E), place)).astype(BF16))
    return out


def _attn_prompt_kernel(q_ref, k_ref, v_ref, lf_ref, sga_ref, o_ref,
                        vt_ref, aug_ref, qt_ref, m_ref, acc_ref, st_ref, p_ref, alpha_ref):
    grp = pl.program_id(1)
    qi = pl.program_id(2)
    tq = q_ref.shape[1]
    tk = MXU_DIM
    n_heads = vt_ref.shape[0]
    n_pairs = n_heads // HEADS_PER_BLOCK

    @pl.when((grp == 0) & (qi == 0))
    def _():
        for b, blk in enumerate(_bias_columns(lf_ref[0])):
            aug_ref[b * MXU_DIM:(b + 1) * MXU_DIM, :] = blk

    row = lax.broadcasted_iota(jnp.int32, (LANES, tq), 0)

    @pl.when(qi == 0)
    def _():
        ones = jnp.ones((vt_ref.shape[2] - HEAD_DIM, tk), BF16)
        for h in range(n_heads):
            for c in range(vt_ref.shape[1]):
                vt_ref[h, c, :HEAD_DIM, :] = v_ref[0, h * HEAD_DIM:(h + 1) * HEAD_DIM, c * tk:(c + 1) * tk]
                vt_ref[h, c, HEAD_DIM:, :] = ones
            lo = 3 * (grp * n_heads + h)
            qt_ref[h, LANES:, :] = ((row >= lo) & (row < lo + 3)).astype(BF16)

    for pair in range(n_pairs):
        qt = q_ref[0, :, _lane_chunk(pair)].astype(F32).T
        for half in range(HEADS_PER_BLOCK):
            h = pair * HEADS_PER_BLOCK + half
            own = (row >= half * HEAD_DIM) & (row < (half + 1) * HEAD_DIM)
            qt_ref[h, :LANES, :] = jnp.where(own, qt, 0.0).astype(BF16)

    def scores(h, j, n_sub):
        rows = n_sub * tk
        k0 = pl.multiple_of(j * tk, tk)
        keys = jnp.concatenate([k_ref[0, pl.ds(k0, rows), _lane_chunk(h // HEADS_PER_BLOCK)],
                                aug_ref[pl.ds(k0, rows), :]], axis=1)
        st_ref[h, :rows, :] = _dot(keys, qt_ref[h])

    def softmax(h, n_sub, mask_last, first):
        rows = n_sub * tk
        diag0 = (n_sub - 1) * tk
        for c in range(tq // LANES):
            cols = _lane_chunk(c)

            def block(r):
                st = st_ref[h, r:r + SOFTMAX_ROWS, cols]
                if mask_last and r >= diag0:
                    key = lax.broadcasted_iota(jnp.int32, st.shape, 0) + (r - diag0)
                    qry = lax.broadcasted_iota(jnp.int32, st.shape, 1) + c * LANES
                    st = jnp.where(key <= qry, st, MASK_VALUE)
                return st

            m8 = None
            for r in range(0, rows, SOFTMAX_ROWS):
                bm = jnp.max(block(r).reshape(SOFTMAX_ROWS // 8, 8, LANES), axis=0)
                m8 = bm if m8 is None else jnp.maximum(m8, bm)
            m_new = jnp.max(m8, axis=0, keepdims=True)
            if not first:
                m_prev = m_ref[h, :, cols]
                m_new = jnp.maximum(m_prev, m_new)
                alpha_ref[h, :, cols] = jnp.exp2(m_prev - m_new)
            m_ref[h, :, cols] = m_new
            for r in range(0, rows, SOFTMAX_ROWS):
                p_ref[h, r:r + SOFTMAX_ROWS, cols] = jnp.exp2(block(r) - m_new).astype(BF16)

    def values(h, j, n_sub, first):
        pv = _dot(vt_ref[h, j], p_ref[h, :tk, :])
        for s in range(1, n_sub):
            pv = pv + _dot(vt_ref[h, j + s], p_ref[h, s * tk:(s + 1) * tk, :])
        acc_ref[h] = pv if first else alpha_ref[h] * acc_ref[h] + pv

    def unit(j, n_sub, mask_last):
        first = j == 0
        for h in range(n_heads):
            scores(h, j, n_sub)
        for h in range(n_heads):
            softmax(h, n_sub, mask_last, first)
        for h in range(n_heads):
            values(h, j, n_sub, first)

    for v in range(vt_ref.shape[1]):
        @pl.when(qi == v)
        def _():
            j = 0
            while v + 1 - j > KEY_CHUNKS_PER_UNIT:
                unit(j, KEY_CHUNKS_PER_UNIT, False)
                j += KEY_CHUNKS_PER_UNIT
            unit(j, v + 1 - j, True)

    for pair in range(n_pairs):
        lanes = _lane_chunk(pair)
        ot = []
        for half in range(HEADS_PER_BLOCK):
            acc = acc_ref[pair * HEADS_PER_BLOCK + half]
            ot.append(acc[:HEAD_DIM] / acc[HEAD_DIM:HEAD_DIM + 1])
        o = jnp.concatenate(ot, axis=0).T
        o_ref[0, :, lanes] = (o * sga_ref[0, :, lanes].astype(F32)).astype(BF16)


def _attn_prompt(q, k, v, lf, sga):
    n_seq, seq_len, _ = q.shape
    tq = Q_TILE
    assert tq == MXU_DIM and seq_len % tq == 0
    n_q = seq_len // tq
    n_pairs = ATTN_PAIRS_PER_STEP
    n_heads = n_pairs * HEADS_PER_BLOCK
    width = n_pairs * LANES
    acc_rows = HEAD_DIM + BF16_SUBLANES
    unit_rows = KEY_CHUNKS_PER_UNIT * MXU_DIM
    qspec = pl.BlockSpec((1, tq, width), lambda b, g, qi: (b, qi, g))
    kspec = pl.BlockSpec((1, seq_len, width), lambda b, g, qi: (b, 0, g))
    vspec = pl.BlockSpec((1, width, seq_len), lambda b, g, qi: (b, g, 0))
    return pl.pallas_call(
        _attn_prompt_kernel,
        grid=(n_seq, D_MODEL // width, n_q),
        in_specs=[qspec, kspec, vspec,
                  pl.BlockSpec((1, seq_len, N_HEADS), lambda b, g, qi: (b, 0, 0)), qspec],
        out_specs=qspec,
        out_shape=jax.ShapeDtypeStruct(q.shape, BF16),
        scratch_shapes=[pltpu.VMEM((n_heads, n_q, acc_rows, tq), BF16),
                        pltpu.VMEM((seq_len, LANES), BF16),
                        pltpu.VMEM((n_heads, 2 * LANES, tq), BF16),
                        pltpu.VMEM((n_heads, 1, tq), F32),
                        pltpu.VMEM((n_heads, acc_rows, tq), F32),
                        pltpu.VMEM((n_heads, unit_rows, tq), F32),
                        pltpu.VMEM((n_heads, unit_rows, tq), BF16),
                        pltpu.VMEM((n_heads, 1, tq), F32)],
        compiler_params=pltpu.CompilerParams(dimension_semantics=("arbitrary",) * 3,
                                             vmem_limit_bytes=VMEM_LIMIT_BYTES),
        name="attn_prompt",
    )(q, k, v, lf, sga)


def _attn_sample_kernel(q_ref, kp_ref, vp_ref, kn_ref, vn_ref, lft_ref, sga_ref, o_ref, negc_ref):
    hp = pl.program_id(1)
    tq = q_ref.shape[1]
    past = kp_ref.shape[2]

    @pl.when(hp == 0)
    def _():
        for b, blk in enumerate(_neg_cumsum_blocks(lft_ref[0])):
            negc_ref[:, b * MXU_DIM:(b + 1) * MXU_DIM] = blk

    qq = _stack_heads(q_ref[0])
    nb0 = negc_ref[pl.ds(HEADS_PER_BLOCK * hp, 1), :]
    nb1 = negc_ref[pl.ds(HEADS_PER_BLOCK * hp + 1, 1), :]

    def biased(s, lo, width):
        return s + jnp.concatenate([jnp.broadcast_to(nb0[:, lo:lo + width], (tq, width)),
                                    jnp.broadcast_to(nb1[:, lo:lo + width], (tq, width))], axis=0)

    s_past = biased(_dot(qq, kp_ref[0].astype(BF16)), 0, past)
    s_new = _causal_mask(biased(_dot_nt(qq, kn_ref[0]), past, tq), tq)
    m = jnp.maximum(jnp.max(s_past, axis=1, keepdims=True), jnp.max(s_new, axis=1, keepdims=True))
    p_past = jnp.exp2(s_past - m)
    p_new = jnp.exp2(s_new - m)
    l = jnp.sum(p_past, axis=1, keepdims=True) + jnp.sum(p_new, axis=1, keepdims=True)
    acc = _dot_nt(p_past.astype(BF16), vp_ref[0].astype(BF16)) + _dot(p_new.astype(BF16), vn_ref[0])
    o = _unstack_heads(acc / l, tq)
    o_ref[0] = (o * sga_ref[0].astype(F32)).astype(BF16)


def _attn_sample(q, k_past, v_past, k_new, v_new, lft_all, sga):
    n_seq, tq, _ = q.shape
    past = k_past.shape[2]
    total = lft_all.shape[2]
    qspec = pl.BlockSpec((1, tq, LANES), lambda b, hp: (b, 0, hp))
    pspec = pl.BlockSpec((1, LANES, past), lambda b, hp: (b, hp, 0))
    return pl.pallas_call(
        _attn_sample_kernel,
        grid=(n_seq, N_HEAD_BLOCKS),
        in_specs=[qspec, pspec, pspec, qspec, qspec,
                  pl.BlockSpec((1, N_HEADS, total), lambda b, hp: (b, 0, 0)), qspec],
        out_specs=qspec,
        out_shape=jax.ShapeDtypeStruct(q.shape, BF16),
        scratch_shapes=[pltpu.VMEM((N_HEADS, total), F32)],
        compiler_params=pltpu.CompilerParams(dimension_semantics=("arbitrary",) * 2,
                                             vmem_limit_bytes=VMEM_LIMIT_BYTES),
        name="attn_sample",
    )(q, k_past, v_past, k_new, v_new, lft_all, sga)


def _out_kernel(x_ref, oa_ref, cg_ref, sma_ref, smb_ref, wpa_ref, wpb_ref, wout_ref, y_ref):
    ya = _dot(oa_ref[...], wpa_ref[...])
    yb = _dot(cg_ref[...], wpb_ref[...])
    m = sma_ref[...].astype(F32) * ya + smb_ref[...].astype(F32) * yb
    y_ref[...] = x_ref[...] + _dot(m.astype(BF16), wout_ref[...])


def _out_proj(x2d, oa, cg, sma, smb, p):
    rows = x2d.shape[0]
    tm = min(OUT_ROW_TILE, rows)
    row_spec = pl.BlockSpec((tm, D_MODEL), lambda i: (i, 0))
    weight_spec = pl.BlockSpec((D_MODEL, D_MODEL), lambda i: (0, 0))
    return pl.pallas_call(
        _out_kernel,
        grid=(rows // tm,),
        in_specs=[row_spec] * 5 + [weight_spec] * 3,
        out_specs=row_spec,
        out_shape=jax.ShapeDtypeStruct((rows, D_MODEL), F32),
        compiler_params=pltpu.CompilerParams(dimension_semantics=("arbitrary",),
                                             vmem_limit_bytes=VMEM_LIMIT_BYTES),
        name="out_proj",
    )(x2d, oa, cg, sma, smb, p["wpa"], p["wpb"], p["wout"])


def _lane_chunked(a):
    return a.reshape(a.shape[0], N_LANE_CHUNKS, LANES).transpose(1, 0, 2)


def _layer_params(norm_g, w_in, b_f, q_g, k_g, w_dw, b_dw, ln_g, ln_b, w_pa, w_pb, w_out):
    off_f = 3 * D_MODEL
    w_f = jnp.pad(w_in[:, off_f:off_f + N_HEADS], ((0, 0), (0, LANES - N_HEADS)))
    w_all = jnp.concatenate([w_in[:, :off_f], w_in[:, off_f + N_HEADS:], w_f], axis=1).astype(BF16)
    head_of_col = jnp.arange(D_MODEL) // HEAD_DIM
    e_mat = (head_of_col[:, None] == jnp.arange(LANES)[None, :]).astype(BF16)
    et_mat = jnp.concatenate([e_mat.T, e_mat.T], axis=0)
    return dict(
        ng=norm_g.reshape(1, D_MODEL), w_all=w_all,
        bf_pad=jnp.pad(b_f, (0, LANES - N_HEADS)).reshape(1, LANES),
        qg=(jnp.tile(q_g, N_HEADS) * (HEAD_DIM ** -0.5 * LOG2_E)).reshape(1, D_MODEL),
        kg=jnp.tile(k_g, N_HEADS).reshape(1, D_MODEL),
        e_mat=e_mat, et_mat=et_mat,
        wdw=_lane_chunked(jnp.pad(w_dw, ((0, HALO - CONV_K), (0, 0)))),
        bdw=_lane_chunked(b_dw.reshape(1, D_MODEL)),
        lng=ln_g.reshape(1, D_MODEL), lnb=ln_b.reshape(1, D_MODEL),
        wpa=w_pa.astype(BF16), wpb=w_pb.astype(BF16), wout=w_out.astype(BF16))


def _mixer_path(x, p, conv_state, cache=None):
    n_seq, seq_len, _ = x.shape
    x2d = x.reshape(n_seq * seq_len, D_MODEL)
    time_minor = cache is None
    q, k, v, kb, vb, lf, lft, sga, cg, utail, sma, smb = _inproj(x2d, seq_len, conv_state, p, time_minor)
    seq3 = lambda a: a.reshape(n_seq, seq_len, D_MODEL)
    if cache is None:
        oa = _attn_prompt(seq3(q), seq3(kb), vb, lf.reshape(n_seq, seq_len, N_HEADS), seq3(sga))
        heads = lambda a: a.reshape(n_seq, N_HEADS, HEAD_DIM, seq_len).transpose(0, 3, 1, 2)
        lf_out = lft.transpose(0, 2, 1)
    else:
        k_past, v_past, lf_past = cache
        past = k_past.shape[1]
        channel_major = lambda a: a.transpose(0, 2, 3, 1).reshape(n_seq, D_MODEL, past)
        n_tiles, _, tm = lft.shape
        lft_new = lft.reshape(n_tiles, N_HEADS, tm // seq_len, seq_len).transpose(0, 2, 1, 3)
        lft_new = lft_new.reshape(n_seq, N_HEADS, seq_len)
        pad = -(past + seq_len) % MXU_DIM
        lft_all = jnp.pad(jnp.concatenate([lf_past.transpose(0, 2, 1), lft_new], axis=2),
                          ((0, 0), (0, 0), (0, pad)))
        oa = _attn_sample(seq3(q), channel_major(k_past), channel_major(v_past),
                          seq3(kb), seq3(vb), lft_all, seq3(sga))
        heads = lambda a: a.reshape(n_seq, seq_len, N_HEADS, HEAD_DIM)
        lf_out = lf.reshape(n_seq, seq_len, N_HEADS)
    y = _out_proj(x2d, oa.reshape(x2d.shape), cg, sma, smb, p)
    return y.reshape(x.shape), heads(k), heads(v), lf_out, utail[:, HALO - CONV_HIST:, :]


def kernel(x_prompt, x_sample, cache_k, cache_v, cache_logf, state_conv, norm_g, w_in, b_f, q_g, k_g,
           w_dw, b_dw, ln_g, ln_b, w_pa, w_pb, w_out):
    depth = w_in.shape[0]
    xp, xs = x_prompt, x_sample
    outs = [[] for _ in range(8)]
    for l in range(depth):
        p = _layer_params(norm_g[l], w_in[l], b_f[l], q_g[l], k_g[l], w_dw[l], b_dw[l], ln_g[l], ln_b[l],
                          w_pa[l], w_pb[l], w_out[l])
        fresh_state = jnp.zeros((xp.shape[0], HALO, D_MODEL), F32)
        xp, kp, vp, fp, cp = _mixer_path(xp, p, fresh_state)
        state = jnp.pad(state_conv[l].astype(F32), ((0, 0), (HALO - CONV_HIST, 0), (0, 0)))
        xs, ks, vs, fs, cs = _mixer_path(xs, p, state, cache=(cache_k[l], cache_v[l], cache_logf[l].astype(F32)))
        for lst, val in zip(outs, (kp, vp, fp, cp, ks, vs, fs, cs)):
            lst.append(val)
    return (xp, xs) + tuple(jnp.stack(o) for o in outs)
```
